```python
import math
import jax
import jax.numpy as jnp
from jax import lax
import numpy as np

D_MODEL = 2048
BATCH = 2
SEQ = 4096
DEPTH = 4
DEC_BATCH = 8
DEC_SEQ = 4
PAST_LEN = 16384
PAGE_SIZE = 128

HEAD_DIM = 128
BRANCH_DIM = D_MODEL // 4
POOL_GROUPS = 4
POOL_DIM = BRANCH_DIM
POOL_GROUP_DIM = POOL_DIM // POOL_GROUPS
POOL_WINDOWS = (2, 4, 8, 16)
POOL_BUF = 15
MOBA_HEADS = BRANCH_DIM // HEAD_DIM
MOBA_DIM = MOBA_HEADS * HEAD_DIM
MOBA_BLOCK = 256
MOBA_TOPK = 3
MOBA_Q_CHUNK = 64
ROPE_THETA = 500000.0
ROPE_DIM = HEAD_DIM // 4
CONV_DIM = BRANCH_DIM
CONV_WIDTH = 31
CONV_BUF = CONV_WIDTH - 1
SB_HEADS = BRANCH_DIM // HEAD_DIM
SB_DIM = SB_HEADS * HEAD_DIM
SB_Q_BLOCK = 128
N_BRANCH = 4
D_FF = 5632
N_EXPERTS = 8
TOP_K = 2
N_DENSE = (DEPTH + 1) // 2
N_MOE = DEPTH // 2
ALPHA = (2.0 * DEPTH) ** 0.25
BETA = (8.0 * DEPTH) ** -0.25
LN_EPS = 1e-5
N_IN = POOL_DIM + 3 * MOBA_DIM + 2 * CONV_DIM + 3 * SB_DIM + N_BRANCH * D_MODEL

kernel_name = 'hybrid_pool_moba_conv_stickbreak_decoder_step'


def layer_norm(x, g=None, b=None):
    xf = x.astype(jnp.float32)
    mu = jnp.mean(xf, axis=-1, keepdims=True)
    var = jnp.mean(jnp.square(xf - mu), axis=-1, keepdims=True)
    y = (xf - mu) * lax.rsqrt(var + LN_EPS)
    if g is not None:
        y = y * g.astype(jnp.float32) + b.astype(jnp.float32)
    return y.astype(x.dtype)


def rotary(x, pos):
    half = ROPE_DIM // 2
    inv_freq = ROPE_THETA ** (-jnp.arange(half, dtype=jnp.float32) / half)
    ang = pos.astype(jnp.float32)[:, None] * inv_freq[None, :]
    cos = jnp.cos(ang)[None, :, None, :]
    sin = jnp.sin(ang)[None, :, None, :]
    xr = x[..., :ROPE_DIM].astype(jnp.float32)
    x1, x2 = xr[..., :half], xr[..., half:]
    rot = jnp.concatenate([x1 * cos - x2 * sin, x2 * cos + x1 * sin], axis=-1)
    return jnp.concatenate([rot.astype(x.dtype), x[..., ROPE_DIM:]], axis=-1)


def chunk_size(length, c):
    return c if length % c == 0 else length


def pool_mix(p, buf, pos, w_grp, scale):
    B, L, _ = p.shape
    xp = jnp.concatenate([buf, p], axis=1).astype(jnp.float32)
    cs = jnp.pad(jnp.cumsum(xp, axis=1), ((0, 0), (1, 0), (0, 0)))
    means = []
    for g, w in enumerate(POOL_WINDOWS):
        ch = slice(g * POOL_GROUP_DIM, (g + 1) * POOL_GROUP_DIM)
        hi = cs[:, POOL_BUF + 1:POOL_BUF + 1 + L, ch]
        lo = cs[:, POOL_BUF + 1 - w:POOL_BUF + 1 - w + L, ch]
        cnt = jnp.minimum(pos + 1, w).astype(jnp.float32)[None, :, None]
        means.append((hi - lo) / cnt)
    d = (jnp.concatenate(means, axis=-1) - xp[:, POOL_BUF:]).astype(p.dtype)
    d = d.reshape(B, L, POOL_GROUPS, POOL_GROUP_DIM)
    y = jnp.einsum('blgc,gcd->blgd', d, w_grp).reshape(B, L, POOL_DIM)
    return y * scale


def conv_module(a, buf, w, b, g, beta):
    h = a[..., :CONV_DIM] * jax.nn.sigmoid(a[..., CONV_DIM:])
    hist = jnp.concatenate([buf, h], axis=1)
    y = lax.conv_general_dilated(hist, w[:, None, :], window_strides=(1,), padding='VALID',
                                 dimension_numbers=('NWC', 'WIO', 'NWC'),
                                 feature_group_count=CONV_DIM) + b
    y = jax.nn.silu(layer_norm(y, g, beta))
    return y, hist[:, -CONV_BUF:]


def moba_attend(q, k, v, q_pos, q_chunk):
    B, Lq, H, Dh = q.shape
    Lk = k.shape[1]
    n_blk = -(-Lk // MOBA_BLOCK)
    pad = n_blk * MOBA_BLOCK - Lk
    kb = jnp.pad(k, ((0, 0), (0, pad), (0, 0), (0, 0))).reshape(B, n_blk, MOBA_BLOCK, H, Dh)
    vb = jnp.pad(v, ((0, 0), (0, pad), (0, 0), (0, 0))).reshape(B, n_blk, MOBA_BLOCK, H, Dh)
    k_mean = jnp.mean(kb.astype(jnp.float32), axis=2)
    kb = kb.transpose(0, 3, 1, 2, 4)
    vb = vb.transpose(0, 3, 1, 2, 4)
    n_sel = min(MOBA_TOPK, n_blk)
    b_idx = jnp.arange(B)[:, None, None, None]
    h_idx = jnp.arange(H)[None, :, None, None]
    offs = jnp.arange(MOBA_BLOCK)
    scale = Dh ** -0.5

    def attend_chunk(args):
        qc, pc = args
        Qc = pc.shape[0]
        own = pc // MOBA_BLOCK
        score = jnp.einsum('bqhd,bnhd->bhqn', qc.astype(jnp.float32), k_mean)
        fully_past = jnp.arange(n_blk)[None, :] < own[:, None]
        score = jnp.where(fully_past[None, None], score, -jnp.inf)
        _, top = lax.top_k(score, n_sel)
        idx = jnp.concatenate([top, jnp.broadcast_to(own[None, None, :, None], (B, H, Qc, 1))], axis=-1)
        slot_ok = jnp.concatenate([jnp.arange(n_sel)[None, :] < own[:, None],
                                   jnp.ones((Qc, 1), dtype=bool)], axis=-1)
        kg = kb[b_idx, h_idx, idx]
        vg = vb[b_idx, h_idx, idx]
        logits = jnp.einsum('bqhd,bhqskd->bhqsk', qc, kg, preferred_element_type=jnp.float32) * scale
        kpos = idx[..., None] * MOBA_BLOCK + offs
        mask = slot_ok[None, None, :, :, None] & (kpos <= pc[None, None, :, None, None])
        logits = jnp.where(mask, logits, -jnp.inf)
        n_slot = idx.shape[-1]
        p = jax.nn.softmax(logits.reshape(B, H, Qc, n_slot * MOBA_BLOCK), axis=-1).reshape(logits.shape)
        return jnp.einsum('bhqsk,bhqskd->bqhd', p.astype(v.dtype), vg)

    n_chunk = Lq // q_chunk
    qs = q.reshape(B, n_chunk, q_chunk, H, Dh).transpose(1, 0, 2, 3, 4)
    ps = q_pos.reshape(n_chunk, q_chunk)
    out = lax.map(attend_chunk, (qs, ps))
    return out.transpose(1, 0, 2, 3, 4).reshape(B, Lq, H, Dh)


def sb_attend(q, k, v, q_pos, q_block):
    B, Lq, H, Dh = q.shape
    Lk = k.shape[1]
    k_pos = jnp.arange(Lk)
    scale = Dh ** -0.5

    def attend_block(args):
        qb, pb = args
        z = jnp.einsum('bqhd,bkhd->bhqk', qb, k, preferred_element_type=jnp.float32) * scale
        causal = (k_pos[None, :] < pb[:, None])[None, None]
        log_1m = jnp.where(causal, jax.nn.log_sigmoid(-z), 0.0)
        later = lax.cumsum(log_1m, axis=3, reverse=True) - log_1m
        a = jnp.where(causal, jnp.exp(jax.nn.log_sigmoid(z) + later), 0.0)
        return jnp.einsum('bhqk,bkhd->bqhd', a.astype(v.dtype), v)

    n_blk = Lq // q_block
    qs = q.reshape(B, n_blk, q_block, H, Dh).transpose(1, 0, 2, 3, 4)
    ps = q_pos.reshape(n_blk, q_block)
    out = lax.map(attend_block, (qs, ps))
    return out.transpose(1, 0, 2, 3, 4).reshape(B, Lq, H, Dh)


def swiglu(x, w1, w3, w2):
    return (jax.nn.silu(x @ w1) * (x @ w3)) @ w2


def moe_ffn(x, router, router_b, w1, w3, w2):
    logits = jnp.einsum('bld,de->ble', x, router, preferred_element_type=jnp.float32) + router_b.astype(jnp.float32)
    top_v, top_i = lax.top_k(logits, TOP_K)
    top_g = jax.nn.softmax(top_v, axis=-1)
    gate = jnp.sum(jax.nn.one_hot(top_i, N_EXPERTS, dtype=jnp.float32) * top_g[..., None], axis=-2).astype(x.dtype)
    y = jnp.zeros_like(x)
    for e in range(N_EXPERTS):
        y = y + gate[..., e:e + 1] * swiglu(x, w1[e], w3[e], w2[e])
    return y


def setup_inputs(seed: int = 0) -> dict:
    key = jax.random.key(seed)
    keys = iter(jax.random.split(key, 48))
    f32 = jnp.float32

    def nrm(shape, std):
        return jax.random.normal(next(keys), shape, f32) * std

    n_pages = PAST_LEN // PAGE_SIZE
    n_used = DEC_BATCH * n_pages
    n_pool = n_used + n_used // 4
    mcache = (DEPTH, n_pool, PAGE_SIZE, MOBA_HEADS, HEAD_DIM)
    scache = (DEPTH, n_pool, PAGE_SIZE, SB_HEADS, HEAD_DIM)
    d_in = D_MODEL ** -0.5
    b_in = BRANCH_DIM ** -0.5
    f_in = D_FF ** -0.5
    inputs = {}
    inputs['x_prompt'] = nrm((BATCH, SEQ, D_MODEL), 1.0)
    inputs['x_sample'] = nrm((DEC_BATCH, DEC_SEQ, D_MODEL), 1.0)
    inputs['cache_moba_k'] = nrm(mcache, 1.0)
    inputs['cache_moba_v'] = nrm(mcache, 1.0)
    inputs['cache_sb_k'] = nrm(scache, 1.0)
    inputs['cache_sb_v'] = nrm(scache, 1.0)
    inputs['state_pool'] = nrm((DEPTH, DEC_BATCH, POOL_BUF, POOL_DIM), 1.0)
    inputs['state_conv'] = nrm((DEPTH, DEC_BATCH, CONV_BUF, CONV_DIM), 0.5)
    inputs['page_table'] = jax.random.permutation(next(keys), n_pool)[:n_used].reshape(DEC_BATCH, n_pages).astype(jnp.int32)
    inputs['c_prompt'] = nrm((BATCH, D_MODEL), 1.0)
    inputs['c_sample'] = nrm((DEC_BATCH, D_MODEL), 1.0)
    inputs['w_ada'] = nrm((DEPTH, D_MODEL, 6 * D_MODEL), 0.5 * d_in)
    inputs['b_ada'] = nrm((DEPTH, 6 * D_MODEL), 0.02)
    inputs['w_in'] = nrm((DEPTH, D_MODEL, N_IN), d_in)
    inputs['pool_w'] = nrm((DEPTH, POOL_GROUPS, POOL_GROUP_DIM, POOL_GROUP_DIM), POOL_GROUP_DIM ** -0.5)
    inputs['pool_scale'] = 1.0 + nrm((DEPTH, POOL_DIM), 0.02)
    inputs['pool_proj'] = nrm((DEPTH, POOL_DIM, D_MODEL), b_in)
    inputs['moba_proj'] = nrm((DEPTH, MOBA_DIM, D_MODEL), b_in)
    inputs['conv_w'] = nrm((DEPTH, CONV_WIDTH, CONV_DIM), CONV_WIDTH ** -0.5)
    inputs['conv_b'] = nrm((DEPTH, CONV_DIM), 0.02)
    inputs['conv_ln_g'] = 1.0 + nrm((DEPTH, CONV_DIM), 0.02)
    inputs['conv_ln_b'] = nrm((DEPTH, CONV_DIM), 0.02)
    inputs['conv_pw'] = nrm((DEPTH, CONV_DIM, D_MODEL), b_in)
    inputs['sb_proj'] = nrm((DEPTH, SB_DIM, D_MODEL), b_in)
    inputs['w_out'] = nrm((DEPTH, D_MODEL, D_MODEL), d_in * BETA)
    inputs['ln1_g'] = 1.0 + nrm((DEPTH, D_MODEL), 0.02)
    inputs['ln1_b'] = nrm((DEPTH, D_MODEL), 0.02)
    inputs['ln2_g'] = 1.0 + nrm((DEPTH, D_MODEL), 0.02)
    inputs['ln2_b'] = nrm((DEPTH, D_MODEL), 0.02)
    inputs['ffn_w1'] = nrm((N_DENSE, D_MODEL, D_FF), d_in)
    inputs['ffn_w3'] = nrm((N_DENSE, D_MODEL, D_FF), d_in)
    inputs['ffn_w2'] = nrm((N_DENSE, D_FF, D_MODEL), f_in * BETA)
    inputs['moe_router'] = nrm((N_MOE, D_MODEL, N_EXPERTS), d_in)
    inputs['moe_router_b'] = nrm((N_MOE, N_EXPERTS), 0.01)
    inputs['moe_w1'] = nrm((N_MOE, N_EXPERTS, D_MODEL, D_FF), d_in)
    inputs['moe_w3'] = nrm((N_MOE, N_EXPERTS, D_MODEL, D_FF), d_in)
    inputs['moe_w2'] = nrm((N_MOE, N_EXPERTS, D_FF, D_MODEL), f_in * BETA)
    return inputs


def reference(x_prompt, x_sample, cache_moba_k, cache_moba_v, cache_sb_k, cache_sb_v, state_pool, state_conv,
              page_table, c_prompt, c_sample, w_ada, b_ada, w_in, pool_w, pool_scale, pool_proj, moba_proj,
              conv_w, conv_b, conv_ln_g, conv_ln_b, conv_pw, sb_proj, w_out, ln1_g, ln1_b, ln2_g, ln2_b,
              ffn_w1, ffn_w3, ffn_w2, moe_router, moe_router_b, moe_w1, moe_w3, moe_w2):
    widths = (POOL_DIM, MOBA_DIM, MOBA_DIM, MOBA_DIM, 2 * CONV_DIM, SB_DIM, SB_DIM, SB_DIM)
    cuts = []
    acc = 0
    for w in widths:
        acc += w
        cuts.append(acc)

    def gather_pages(cache):
        g = cache[page_table]
        return g.reshape(page_table.shape[0], page_table.shape[1] * PAGE_SIZE, cache.shape[2], cache.shape[3])

    def layer(l, x, c, pos, pool_buf, conv_buf, past):
        B, L, _ = x.shape
        mod = jax.nn.silu(c) @ w_ada[l] + b_ada[l]
        sh1, sc1, g1, sh2, sc2, g2 = jnp.split(mod[:, None, :], 6, axis=-1)
        u = layer_norm(x) * (1 + sc1) + sh1
        proj = u @ w_in[l]
        p_in, mq, mk, mv, c_in, sq, sk, sv, g_logit = jnp.split(proj, cuts, axis=-1)

        def heads(t):
            return t.reshape(B, L, -1, HEAD_DIM)

        ya = pool_mix(p_in, pool_buf, pos, pool_w[l], pool_scale[l]) @ pool_proj[l]
        hc, new_conv = conv_module(c_in, conv_buf, conv_w[l], conv_b[l], conv_ln_g[l], conv_ln_b[l])
        yc = hc @ conv_pw[l]
        mq = rotary(heads(mq), pos)
        mk = rotary(heads(mk), pos)
        mv = heads(mv)
        sq, sk, sv = heads(sq), heads(sk), heads(sv)
        if past is None:
            mk_all, mv_all, sk_all, sv_all = mk, mv, sk, sv
        else:
            mk_all = jnp.concatenate([past[0], mk], axis=1)
            mv_all = jnp.concatenate([past[1], mv], axis=1)
            sk_all = jnp.concatenate([past[2], sk], axis=1)
            sv_all = jnp.concatenate([past[3], sv], axis=1)
        yb = moba_attend(mq, mk_all, mv_all, pos, chunk_size(L, MOBA_Q_CHUNK)).reshape(B, L, MOBA_DIM) @ moba_proj[l]
        yd = sb_attend(sq, sk_all, sv_all, pos, chunk_size(L, SB_Q_BLOCK)).reshape(B, L, SB_DIM) @ sb_proj[l]
        gates = jax.nn.sigmoid(g_logit.reshape(B, L, N_BRANCH, D_MODEL))
        m = gates[:, :, 0] * ya + gates[:, :, 1] * yb + gates[:, :, 2] * yc + gates[:, :, 3] * yd
        x = layer_norm(ALPHA * x + g1 * (m @ w_out[l]), ln1_g[l], ln1_b[l])
        u2 = layer_norm(x) * (1 + sc2) + sh2
        if l % 2 == 0:
            j = l // 2
            f = swiglu(u2, ffn_w1[j], ffn_w3[j], ffn_w2[j])
        else:
            j = l // 2
            f = moe_ffn(u2, moe_router[j], moe_router_b[j], moe_w1[j], moe_w3[j], moe_w2[j])
        x = layer_norm(ALPHA * x + g2 * f, ln2_g[l], ln2_b[l])
        new_pool = jnp.concatenate([pool_buf, p_in], axis=1)[:, -POOL_BUF:]
        return x, (mk, mv, sk, sv, new_pool, new_conv)

    bp, lp = x_prompt.shape[0], x_prompt.shape[1]
    pos_p = jnp.arange(lp, dtype=jnp.int32)
    zero_pool = jnp.zeros((bp, POOL_BUF, POOL_DIM), x_prompt.dtype)
    zero_conv = jnp.zeros((bp, CONV_BUF, CONV_DIM), x_prompt.dtype)
    xp = x_prompt
    sp = [[] for _ in range(6)]
    for l in range(DEPTH):
        xp, st = layer(l, xp, c_prompt, pos_p, zero_pool, zero_conv, None)
        for i in range(6):
            sp[i].append(st[i])
    y_prompt = xp

    past_len = page_table.shape[1] * PAGE_SIZE
    pos_s = past_len + jnp.arange(x_sample.shape[1], dtype=jnp.int32)
    xs = x_sample
    ss = [[] for _ in range(6)]
    for l in range(DEPTH):
        past = (gather_pages(cache_moba_k[l]), gather_pages(cache_moba_v[l]),
                gather_pages(cache_sb_k[l]), gather_pages(cache_sb_v[l]))
        xs, st = layer(l, xs, c_sample, pos_s, state_pool[l], state_conv[l], past)
        for i in range(6):
            ss[i].append(st[i])
    y_sample = xs

    p_moba_k = jnp.stack(sp[0])
    p_moba_v = jnp.stack(sp[1])
    p_sb_k = jnp.stack(sp[2])
    p_sb_v = jnp.stack(sp[3])
    p_pool = jnp.stack(sp[4])
    p_conv = jnp.stack(sp[5])
    s_moba_k = jnp.stack(ss[0])
    s_moba_v = jnp.stack(ss[1])
    s_sb_k = jnp.stack(ss[2])
    s_sb_v = jnp.stack(ss[3])
    s_pool = jnp.stack(ss[4])
    s_conv = jnp.stack(ss[5])
    return (y_prompt, y_sample, p_moba_k, p_moba_v, p_sb_k, p_sb_v, p_pool, p_conv,
            s_moba_k, s_moba_v, s_sb_k, s_sb_v, s_pool, s_conv)
```

```python
import functools
import math

import jax
import jax.numpy as jnp
from jax import lax
from jax.experimental import pallas as pl
from jax.experimental.pallas import tpu as pltpu

F32 = jnp.float32
BF16 = jnp.bfloat16
I32 = jnp.int32

D_MODEL = 2048
DEPTH = 4
PAGE_SIZE = 128
HEAD_DIM = 128
BRANCH_DIM = 512
N_HEADS = BRANCH_DIM // HEAD_DIM
POOL_WINDOWS = (2, 4, 8, 16)
POOL_BUF = 15
POOL_HALO = 16
MOBA_BLOCK = 256
MOBA_TOPK = 3
ROPE_THETA = 500000.0
ROPE_DIM = HEAD_DIM // 4
CONV_WIDTH = 31
CONV_BUF = CONV_WIDTH - 1
CONV_HALO = 32
N_BRANCH = 4
D_FF = 5632
N_EXPERTS = 8
ALPHA = (2.0 * DEPTH) ** 0.25
LN_EPS = 1e-5
N_PROJ = 9 * BRANCH_DIM
ATT_SCALE = HEAD_DIM ** -0.5

SB_EXIT = -104.0
NEG_BIG = -1e30

VMEM_LIMIT_V7X = 56 * 1024 * 1024
LANES = 128


def _params(n_axes):
    return pltpu.CompilerParams(dimension_semantics=("arbitrary",) * n_axes,
                                vmem_limit_bytes=VMEM_LIMIT_V7X)


def _ln(x):
    mu = jnp.mean(x, axis=-1, keepdims=True)
    xc = x - mu
    var = jnp.mean(xc * xc, axis=-1, keepdims=True)
    return xc * lax.rsqrt(var + LN_EPS)


def _dot(a, b):
    return jnp.dot(a, b, preferred_element_type=F32)


def _dot_nt(a, b):
    return lax.dot_general(a, b, (((1,), (1,)), ((), ())), preferred_element_type=F32)


def _split2(x):
    hi = x.astype(BF16)
    lo = (x - hi.astype(F32)).astype(BF16)
    return hi, lo


def _split3(x):
    hi = x.astype(BF16)
    r = x - hi.astype(F32)
    mid = r.astype(BF16)
    lo = (r - mid.astype(F32)).astype(BF16)
    return hi, mid, lo


def _dot_nt_precise(a, b):
    ah, al = _split2(a)
    bh, bl = _split2(b)
    return _dot_nt(ah, bh) + (_dot_nt(ah, bl) + _dot_nt(al, bh))


def _dot_exact_rhs(a, b_bf16):
    h, m, l = _split3(a)
    return _dot(h, b_bf16) + (_dot(m, b_bf16) + _dot(l, b_bf16))


def _softplus_neg_abs(z):
    return jnp.log1p(jnp.exp(-jnp.abs(z)))


def _ada_kernel(c_ref, w_ref, b_ref, o_ref):
    c = c_ref[...]
    s = (c * jax.nn.sigmoid(c)).astype(BF16)
    o_ref[...] = _dot(s, w_ref[...].astype(BF16)) + b_ref[...]


def ada_mod(c_all, w_ada, b_ada):
    rows = c_all.shape[0]
    tn = 1024
    n = w_ada.shape[-1]
    return pl.pallas_call(
        _ada_kernel,
        grid=(DEPTH, n // tn),
        in_specs=[pl.BlockSpec((rows, D_MODEL), lambda l, j: (0, 0)),
                  pl.BlockSpec((None, D_MODEL, tn), lambda l, j: (l, 0, j)),
                  pl.BlockSpec((None, 1, tn), lambda l, j: (l, 0, j))],
        out_specs=pl.BlockSpec((None, rows, tn), lambda l, j: (l, 0, j)),
        out_shape=jax.ShapeDtypeStruct((DEPTH, rows, n), F32),
        compiler_params=_params(2),
        name="ada_mod",
    )(c_all, w_ada, b_ada.reshape(DEPTH, 1, n))


def _mod_spec(mod, tm, tiles_per_group):
    if mod.shape[1] == 1:
        return pl.BlockSpec((None, 1, D_MODEL), lambda i: (i // tiles_per_group, 0, 0))
    return pl.BlockSpec((None, tm, D_MODEL), lambda i: (0, i, 0))


def _ln_mod_kernel(x_ref, sc_ref, sh_ref, o_ref):
    o_ref[...] = (_ln(x_ref[...]) * (1.0 + sc_ref[...]) + sh_ref[...]).astype(o_ref.dtype)


def ln_mod(x, sc, sh, rows_per_group):
    m = x.shape[0]
    tm = min(m, 256)
    tpg = max(rows_per_group // tm, 1)
    row = pl.BlockSpec((tm, D_MODEL), lambda i: (i, 0))
    return pl.pallas_call(
        _ln_mod_kernel,
        grid=(m // tm,),
        in_specs=[row, _mod_spec(sc, tm, tpg), _mod_spec(sh, tm, tpg)],
        out_specs=row,
        out_shape=jax.ShapeDtypeStruct((m, D_MODEL), BF16),
        compiler_params=_params(1),
        name="ln_mod",
    )(x, sc, sh)


def _pack_bf16_pairs(u):
    half = u.shape[1] // 2
    a = pltpu.bitcast(u[:, :half].astype(BF16).astype(F32), jnp.uint32)
    b = pltpu.bitcast(u[:, half:].astype(BF16).astype(F32), jnp.uint32)
    return (a & jnp.uint32(0xFFFF0000)) | (b >> 16)


def _route_top2(u, rh_ref, rl_ref, rb_ref):
    uh, ul = _split2(u)
    lg = _dot(uh, rh_ref[...]) + (_dot(uh, rl_ref[...]) + _dot(ul, rh_ref[...])) + rb_ref[...]
    lane = lax.broadcasted_iota(I32, lg.shape, 1)
    m1 = jnp.max(lg, axis=-1, keepdims=True)
    i1 = jnp.min(jnp.where(lg == m1, lane, LANES), axis=-1, keepdims=True)
    lg2 = jnp.where(lane == i1, -jnp.inf, lg)
    m2 = jnp.max(lg2, axis=-1, keepdims=True)
    i2 = jnp.min(jnp.where(lg2 == m2, lane, LANES), axis=-1, keepdims=True)
    e2 = jnp.exp(m2 - m1)
    g1 = 1.0 / (1.0 + e2)
    g2 = e2 / (1.0 + e2)
    out = jnp.where(lane == 0, i1.astype(F32), 0.0)
    out = jnp.where(lane == 1, i2.astype(F32), out)
    out = jnp.where(lane == 2, g1, out)
    out = jnp.where(lane == 3, g2, out)
    return out


def _resid_ln_kernel(*refs, mode):
    x_ref, y_ref, g_ref, lg_ref, lb_ref = refs[:5]
    t = ALPHA * x_ref[...] + g_ref[...] * y_ref[...]
    xn = _ln(t) * lg_ref[...] + lb_ref[...]
    if mode == "last":
        refs[5][...] = xn
        return
    sc_ref, sh_ref = refs[5:7]
    u = _ln(xn) * (1.0 + sc_ref[...]) + sh_ref[...]
    if mode == "next":
        xo_ref, uo_ref = refs[7:9]
        xo_ref[...] = xn
        uo_ref[...] = u.astype(BF16)
    else:
        rh_ref, rl_ref, rb_ref, xo_ref, up_ref, rt_ref = refs[7:13]
        xo_ref[...] = xn
        up_ref[...] = _pack_bf16_pairs(u)
        rt_ref[...] = _route_top2(u, rh_ref, rl_ref, rb_ref)


def resid_ln(x, y, g, ln_g, ln_b, rows_per_group, sc=None, sh=None, router=None):
    m = x.shape[0]
    tm = min(m, 256)
    tpg = max(rows_per_group // tm, 1)
    row = pl.BlockSpec((tm, D_MODEL), lambda i: (i, 0))
    vec = pl.BlockSpec((1, D_MODEL), lambda i: (0, 0))
    ins = [x, y, g, ln_g.reshape(1, D_MODEL), ln_b.reshape(1, D_MODEL)]
    in_specs = [row, row, _mod_spec(g, tm, tpg), vec, vec]
    x_out = jax.ShapeDtypeStruct((m, D_MODEL), F32)
    if sc is None:
        mode, out_shape, out_specs = "last", x_out, row
    else:
        ins += [sc, sh]
        in_specs += [_mod_spec(sc, tm, tpg), _mod_spec(sh, tm, tpg)]
        if router is None:
            mode = "next"
            out_shape = (x_out, jax.ShapeDtypeStruct((m, D_MODEL), BF16))
            out_specs = (row, row)
        else:
            mode = "route"
            rw, rb = router
            rw_pad = jnp.zeros((D_MODEL, LANES), F32).at[:, :N_EXPERTS].set(rw)
            rhi = rw_pad.astype(BF16)
            rlo = (rw_pad - rhi.astype(F32)).astype(BF16)
            rb_pad = jnp.full((1, LANES), -jnp.inf, F32).at[0, :N_EXPERTS].set(rb)
            ins += [rhi, rlo, rb_pad]
            full = pl.BlockSpec((D_MODEL, LANES), lambda i: (0, 0))
            in_specs += [full, full, pl.BlockSpec((1, LANES), lambda i: (0, 0))]
            out_shape = (x_out, jax.ShapeDtypeStruct((m, D_MODEL // 2), jnp.uint32),
                         jax.ShapeDtypeStruct((m, LANES), F32))
            out_specs = (row, pl.BlockSpec((tm, D_MODEL // 2), lambda i: (i, 0)),
                         pl.BlockSpec((tm, LANES), lambda i: (i, 0)))
    return pl.pallas_call(
        functools.partial(_resid_ln_kernel, mode=mode),
        grid=(m // tm,),
        in_specs=in_specs, out_specs=out_specs, out_shape=out_shape,
        compiler_params=_params(1),
        name="resid_ln_" + mode,
    )(*ins)


def _mm_kernel(x_ref, w_ref, o_ref, wb_ref):
    @pl.when(pl.program_id(1) == 0)
    def _():
        wb_ref[...] = w_ref[...].astype(BF16)
    o_ref[...] = _dot(x_ref[...], wb_ref[...]).astype(o_ref.dtype)


def matmul(x, w, layer, *, col0=0, n=None, tn=512, tm=512, out_dtype=F32):
    m, k = x.shape
    n = w.shape[-1] if n is None else n
    tm = min(m, tm)
    cb = col0 // tn
    return pl.pallas_call(
        _mm_kernel,
        grid=(n // tn, m // tm),
        in_specs=[pl.BlockSpec((tm, k), lambda j, i: (i, 0)),
                  pl.BlockSpec((None, k, tn), lambda j, i: (layer, 0, cb + j))],
        out_specs=pl.BlockSpec((tm, tn), lambda j, i: (i, j)),
        out_shape=jax.ShapeDtypeStruct((m, n), out_dtype),
        scratch_shapes=[pltpu.VMEM((k, tn), BF16)],
        compiler_params=_params(2),
        name="matmul",
    )(x, w)


def _pool_kernel(x_ref, init_ref, wg_ref, scale_ref, o_ref, halo_ref, ext_ref, *, tm, n_tiles, pos0):
    i = pl.program_id(1)

    @pl.when(i == 0)
    def _():
        halo_ref[...] = init_ref[...]

    x = x_ref[...]
    ext_ref[0:POOL_HALO, :] = halo_ref[...]
    ext_ref[POOL_HALO:POOL_HALO + tm, :] = x
    if n_tiles > 1:
        halo_ref[...] = x[tm - POOL_HALO:, :]
    pos = pos0 + i * tm + lax.broadcasted_iota(I32, (tm, 1), 0)
    for g, w in enumerate(POOL_WINDOWS):
        cols = pl.ds(g * LANES, LANES)
        s = ext_ref[pl.ds(POOL_HALO, tm), cols]
        for j in range(1, w):
            s = s + ext_ref[pl.ds(POOL_HALO - j, tm), cols]
        cnt = jnp.minimum(pos + 1, w).astype(F32)
        d = s / cnt - x[:, g * LANES:(g + 1) * LANES]
        y = _dot(d.astype(BF16), wg_ref[g].astype(BF16))
        o_ref[:, g * LANES:(g + 1) * LANES] = (y * scale_ref[:, g * LANES:(g + 1) * LANES]).astype(o_ref.dtype)


def pool_branch(proj, init, w_grp, scale, *, pos0):
    n_seq, seq_len, _ = proj.shape
    tm = min(seq_len, 256)
    n_tiles = seq_len // tm
    return pl.pallas_call(
        functools.partial(_pool_kernel, tm=tm, n_tiles=n_tiles, pos0=pos0),
        grid=(n_seq, n_tiles),
        in_specs=[pl.BlockSpec((None, tm, BRANCH_DIM), lambda b, i: (b, i, 0)),
                  pl.BlockSpec((None, POOL_HALO, BRANCH_DIM), lambda b, i: (b, 0, 0)),
                  pl.BlockSpec((len(POOL_WINDOWS), LANES, LANES), lambda b, i: (0, 0, 0)),
                  pl.BlockSpec((1, BRANCH_DIM), lambda b, i: (0, 0))],
        out_specs=pl.BlockSpec((None, tm, BRANCH_DIM), lambda b, i: (b, i, 0)),
        out_shape=jax.ShapeDtypeStruct((n_seq, seq_len, BRANCH_DIM), BF16),
        scratch_shapes=[pltpu.VMEM((POOL_HALO, BRANCH_DIM), F32),
                        pltpu.VMEM((POOL_HALO + tm, BRANCH_DIM), F32)],
        compiler_params=_params(2),
        name="pool_branch",
    )(proj, init, w_grp, scale.reshape(1, BRANCH_DIM))


def _conv_kernel(a_ref, init_ref, w_ref, b_ref, g_ref, beta_ref, o_ref, h_ref,
                 halo_ref, ext_ref, y_ref, *, tm, n_tiles, rc):
    i = pl.program_id(1)

    @pl.when(i == 0)
    def _():
        halo_ref[...] = init_ref[...]

    a = a_ref[...]
    h = a[:, :BRANCH_DIM] * jax.nn.sigmoid(a[:, BRANCH_DIM:])
    h_ref[...] = h
    ext_ref[0:CONV_HALO, :] = halo_ref[...]
    ext_ref[CONV_HALO:CONV_HALO + tm, :] = h
    if n_tiles > 1:
        halo_ref[...] = h[tm - CONV_HALO:, :]
    off = CONV_HALO - CONV_BUF
    for c in range(BRANCH_DIM // LANES):
        cols = pl.ds(c * LANES, LANES)
        for r0 in range(0, tm, rc):
            acc = jnp.zeros((rc, LANES), F32) + b_ref[:, c * LANES:(c + 1) * LANES]
            for k in range(CONV_WIDTH):
                acc = acc + ext_ref[pl.ds(r0 + off + k, rc), cols] * w_ref[k:k + 1, c * LANES:(c + 1) * LANES]
            y_ref[pl.ds(r0, rc), cols] = acc
    y = _ln(y_ref[...]) * g_ref[...] + beta_ref[...]
    o_ref[...] = (y * jax.nn.sigmoid(y)).astype(o_ref.dtype)


def conv_branch(proj, init, w, b, g, beta):
    n_seq, seq_len, _ = proj.shape
    tm = min(seq_len, 128)
    rc = min(tm, 64)
    n_tiles = seq_len // tm
    col_block = (4 * BRANCH_DIM) // (2 * BRANCH_DIM)
    w_pad = jnp.zeros((CONV_HALO, BRANCH_DIM), F32).at[:CONV_WIDTH].set(w)
    vec = pl.BlockSpec((1, BRANCH_DIM), lambda bb, i: (0, 0))
    row = pl.BlockSpec((None, tm, BRANCH_DIM), lambda bb, i: (bb, i, 0))
    return pl.pallas_call(
        functools.partial(_conv_kernel, tm=tm, n_tiles=n_tiles, rc=rc),
        grid=(n_seq, n_tiles),
        in_specs=[pl.BlockSpec((None, tm, 2 * BRANCH_DIM), lambda bb, i: (bb, i, col_block)),
                  pl.BlockSpec((None, CONV_HALO, BRANCH_DIM), lambda bb, i: (bb, 0, 0)),
                  pl.BlockSpec((CONV_HALO, BRANCH_DIM), lambda bb, i: (0, 0)),
                  vec, vec, vec],
        out_specs=(row, row),
        out_shape=(jax.ShapeDtypeStruct((n_seq, seq_len, BRANCH_DIM), BF16),
                   jax.ShapeDtypeStruct((n_seq, seq_len, BRANCH_DIM), F32)),
        scratch_shapes=[pltpu.VMEM((CONV_HALO, BRANCH_DIM), F32),
                        pltpu.VMEM((CONV_HALO + tm, BRANCH_DIM), F32),
                        pltpu.VMEM((tm, BRANCH_DIM), F32)],
        compiler_params=_params(2),
        name="conv_branch",
    )(proj, init, w_pad, b.reshape(1, -1), g.reshape(1, -1), beta.reshape(1, -1))


def rope_tables(pos):
    half = ROPE_DIM // 2
    inv_freq = ROPE_THETA ** (-jnp.arange(half, dtype=F32) / half)
    ang = pos.astype(F32)[:, None] * inv_freq[None, :]
    cos, sin = jnp.cos(ang), jnp.sin(ang)
    n = pos.shape[0]
    z = jnp.zeros((n, half), F32)
    rest = HEAD_DIM - ROPE_DIM
    cos_t = jnp.concatenate([cos, cos, jnp.ones((n, rest), F32)], axis=1)
    sin_up = jnp.concatenate([-sin, z, jnp.zeros((n, rest), F32)], axis=1)
    sin_dn = jnp.concatenate([z, sin, jnp.zeros((n, rest), F32)], axis=1)
    return cos_t, sin_up, sin_dn


def _rope(x, cos_t, sin_up, sin_dn):
    half = ROPE_DIM // 2
    width = x.shape[1]
    tile = lambda t: jnp.concatenate([t] * (width // HEAD_DIM), axis=1)
    up = pltpu.roll(x, width - half, axis=1)
    dn = pltpu.roll(x, half, axis=1)
    return x * tile(cos_t) + up * tile(sin_up) + dn * tile(sin_dn)


def _rope_kernel(q_ref, k_ref, cos_ref, su_ref, sd_ref, qo_ref, ko_ref, *maybe_km_ref, tm):
    cos_t, su, sd = cos_ref[...], su_ref[...], sd_ref[...]
    qo_ref[...] = _rope(q_ref[...], cos_t, su, sd)
    kr = _rope(k_ref[...], cos_t, su, sd)
    ko_ref[...] = kr
    if maybe_km_ref:
        maybe_km_ref[0][...] = jnp.sum(kr, axis=0, keepdims=True) * (1.0 / tm)


def rope_qk(proj, tables, *, with_means):
    n_seq, seq_len, _ = proj.shape
    tm = min(seq_len, MOBA_BLOCK)
    n_tiles = seq_len // tm
    row = pl.BlockSpec((None, tm, BRANCH_DIM), lambda b, i: (b, i, 0))
    tab = pl.BlockSpec((tm, HEAD_DIM), lambda b, i: (i, 0))
    out_shape = [jax.ShapeDtypeStruct((n_seq, seq_len, BRANCH_DIM), F32)] * 2
    out_specs = [row, row]
    if with_means:
        assert tm == MOBA_BLOCK
        out_shape.append(jax.ShapeDtypeStruct((n_seq, n_tiles, 1, BRANCH_DIM), F32))
        out_specs.append(pl.BlockSpec((None, None, 1, BRANCH_DIM), lambda b, i: (b, i, 0, 0)))
    return pl.pallas_call(
        functools.partial(_rope_kernel, tm=tm),
        grid=(n_seq, n_tiles),
        in_specs=[pl.BlockSpec((None, tm, BRANCH_DIM), lambda b, i: (b, i, 1)),
                  pl.BlockSpec((None, tm, BRANCH_DIM), lambda b, i: (b, i, 2)),
                  tab, tab, tab],
        out_specs=tuple(out_specs), out_shape=tuple(out_shape),
        compiler_params=_params(2),
        name="rope_qk",
    )(proj, proj, *tables)


def _moba_prompt_kernel(q_ref, k_ref, v_ref, km_ref, o_ref, *, n_blk):
    blk = MOBA_BLOCK
    qi = pl.program_id(2)
    qf = q_ref[...]
    s = _dot_nt_precise(qf, km_ref[...])
    n_iota = lax.broadcasted_iota(I32, (1, n_blk), 1)
    valid = n_iota < qi
    sm = jnp.where(valid, s, -jnp.inf)
    rank = jnp.zeros(s.shape, I32)
    for m in range(n_blk):
        col = sm[:, m:m + 1]
        beats = (col > sm) | ((col == sm) & (m < n_iota))
        rank = rank + jnp.where(beats & (m < qi), 1, 0)
    sel = jnp.where((valid & (rank < MOBA_TOPK)) | (n_iota == qi), 1.0, 0.0)

    qb = qf.astype(BF16)
    qpos = qi * blk + lax.broadcasted_iota(I32, (blk, 1), 0)
    krel = lax.broadcasted_iota(I32, (1, blk), 1)

    def body(j, carry):
        m_i, l_i, acc = carry
        k0 = pl.multiple_of(j * blk, blk)
        kb = k_ref[pl.ds(k0, blk), :].astype(BF16)
        vb = v_ref[pl.ds(k0, blk), :].astype(BF16)
        lg = _dot_nt(qb, kb) * ATT_SCALE
        picked = jnp.sum(jnp.where(n_iota == j, sel, 0.0), axis=-1, keepdims=True)
        mask = (picked > 0.0) & (k0 + krel <= qpos)
        lg = jnp.where(mask, lg, NEG_BIG)
        m_new = jnp.maximum(m_i, jnp.max(lg, axis=-1, keepdims=True))
        alpha = jnp.exp(m_i - m_new)
        p = jnp.where(mask, jnp.exp(lg - m_new), 0.0)
        l_new = alpha * l_i + jnp.sum(p, axis=-1, keepdims=True)
        acc = alpha * acc + _dot(p.astype(BF16), vb)
        return m_new, l_new, acc

    init = (jnp.full((blk, 1), NEG_BIG, F32), jnp.zeros((blk, 1), F32), jnp.zeros((blk, HEAD_DIM), F32))
    _, l_f, acc = lax.fori_loop(0, qi + 1, body, init)
    o_ref[...] = (acc / l_f).astype(o_ref.dtype)


def moba_prompt(q_rot, k_rot, proj, k_mean):
    n_seq, seq_len, _ = proj.shape
    n_blk = seq_len // MOBA_BLOCK
    km = k_mean.reshape(n_seq, n_blk, BRANCH_DIM)
    tile = pl.BlockSpec((None, MOBA_BLOCK, HEAD_DIM), lambda b, h, i: (b, i, h))
    return pl.pallas_call(
        functools.partial(_moba_prompt_kernel, n_blk=n_blk),
        grid=(n_seq, N_HEADS, n_blk),
        in_specs=[tile,
                  pl.BlockSpec((None, seq_len, HEAD_DIM), lambda b, h, i: (b, 0, h)),
                  pl.BlockSpec((None, seq_len, HEAD_DIM), lambda b, h, i: (b, 0, 3 * N_HEADS + h)),
                  pl.BlockSpec((None, n_blk, HEAD_DIM), lambda b, h, i: (b, 0, h))],
        out_specs=tile,
        out_shape=jax.ShapeDtypeStruct((n_seq, seq_len, BRANCH_DIM), BF16),
        compiler_params=_params(3),
        name="moba_prompt",
    )(q_rot, k_rot, proj, km)


def _strict_lower(n):
    r = lax.broadcasted_iota(I32, (n, n), 0)
    c = lax.broadcasted_iota(I32, (n, n), 1)
    return jnp.where(r > c, 1.0, 0.0).astype(BF16)


def _sb_block(qb, kb, vb, tri, causal, carry, acc):
    z = _dot_nt(qb, kb) * ATT_SCALE
    sp = _softplus_neg_abs(z)
    log_sig = jnp.minimum(z, 0.0) - sp
    log_1m = jnp.where(causal, -jnp.maximum(z, 0.0) - sp, 0.0)
    later = _dot_exact_rhs(log_1m, tri) + carry
    a = jnp.where(causal, jnp.exp(log_sig + later), 0.0)
    acc = acc + _dot(a.astype(BF16), vb)
    carry = carry + jnp.sum(log_1m, axis=-1, keepdims=True)
    return carry, acc


def _sb_prompt_kernel(q_ref, k_ref, v_ref, o_ref, *, tq, tk):
    qi = pl.program_id(2)
    qb = q_ref[...].astype(BF16)
    qpos = qi * tq + lax.broadcasted_iota(I32, (tq, 1), 0)
    krel = lax.broadcasted_iota(I32, (1, tk), 1)
    tri = _strict_lower(tk)

    def cond(c):
        return (c[0] >= 0) & (c[1] > 0)

    def body(c):
        j, _, carry, acc = c
        k0 = pl.multiple_of(j * tk, tk)
        kb = k_ref[pl.ds(k0, tk), :].astype(BF16)
        vb = v_ref[pl.ds(k0, tk), :].astype(BF16)
        carry, acc = _sb_block(qb, kb, vb, tri, (k0 + krel) < qpos, carry, acc)
        live = (jnp.max(carry) > SB_EXIT).astype(I32)
        return j - 1, live, carry, acc

    n_kb = ((qi + 1) * tq) // tk
    init = (n_kb - 1, jnp.int32(1), jnp.zeros((tq, 1), F32), jnp.zeros((tq, HEAD_DIM), F32))
    o_ref[...] = lax.while_loop(cond, body, init)[3].astype(o_ref.dtype)


def sb_prompt(proj):
    n_seq, seq_len, _ = proj.shape
    tq, tk = 256, 128
    base = 6 * N_HEADS
    seq = lambda off: pl.BlockSpec((None, seq_len, HEAD_DIM), lambda b, h, i: (b, 0, base + off + h))
    return pl.pallas_call(
        functools.partial(_sb_prompt_kernel, tq=tq, tk=tk),
        grid=(n_seq, N_HEADS, seq_len // tq),
        in_specs=[pl.BlockSpec((None, tq, HEAD_DIM), lambda b, h, i: (b, i, base + h)),
                  seq(N_HEADS), seq(2 * N_HEADS)],
        out_specs=pl.BlockSpec((None, tq, HEAD_DIM), lambda b, h, i: (b, i, h)),
        out_shape=jax.ShapeDtypeStruct((n_seq, seq_len, BRANCH_DIM), BF16),
        compiler_params=_params(3),
        name="sb_prompt",
    )(proj, proj, proj)


PAGES_PER_STEP = 8
Q_PAD = 8


def _page_mean_kernel(pt_ref, *refs):
    o_ref = refs[-1]
    for n in range(PAGES_PER_STEP // 2):
        s = jnp.sum(refs[2 * n][...], axis=0, keepdims=True) + jnp.sum(refs[2 * n + 1][...], axis=0, keepdims=True)
        o_ref[n] = s * (1.0 / MOBA_BLOCK)


def page_block_means(cache, layer, page_table):
    n_seq, n_pages = page_table.shape
    c4 = cache.reshape(cache.shape[0], cache.shape[1], PAGE_SIZE, BRANCH_DIM)
    bps = PAGES_PER_STEP // 2
    specs = [pl.BlockSpec((None, None, PAGE_SIZE, BRANCH_DIM),
                          functools.partial(lambda b, s, pt, k: (layer, pt[b, s * PAGES_PER_STEP + k], 0, 0), k=k))
             for k in range(PAGES_PER_STEP)]
    return pl.pallas_call(
        _page_mean_kernel,
        grid_spec=pltpu.PrefetchScalarGridSpec(
            num_scalar_prefetch=1,
            grid=(n_seq, n_pages // PAGES_PER_STEP),
            in_specs=specs,
            out_specs=pl.BlockSpec((None, bps, 1, BRANCH_DIM), lambda b, s, pt: (b, s, 0, 0))),
        out_shape=jax.ShapeDtypeStruct((n_seq, n_pages // 2, 1, BRANCH_DIM), F32),
        compiler_params=_params(2),
        name="page_block_means",
    )(page_table, *([c4] * PAGES_PER_STEP))


def _moba_select_kernel(q_ref, km_ref, o_ref, *, n_blk):
    lane = lax.broadcasted_iota(I32, (Q_PAD, LANES), 1)
    n_iota = lax.broadcasted_iota(I32, (Q_PAD, n_blk), 1)
    out = jnp.zeros((Q_PAD, LANES), I32)
    for h in range(N_HEADS):
        cols = slice(h * HEAD_DIM, (h + 1) * HEAD_DIM)
        s = _dot_nt_precise(q_ref[:, cols], km_ref[:, cols])
        for slot in range(MOBA_TOPK):
            top = jnp.max(s, axis=-1, keepdims=True)
            idx = jnp.min(jnp.where(s == top, n_iota, n_blk), axis=-1, keepdims=True)
            s = jnp.where(n_iota == idx, -jnp.inf, s)
            out = jnp.where(lane == h * MOBA_TOPK + slot, idx, out)
    o_ref[...] = out


def moba_select(q_pad, k_mean):
    n_seq, n_blk = k_mean.shape[0], k_mean.shape[1]
    return pl.pallas_call(
        functools.partial(_moba_select_kernel, n_blk=n_blk),
        grid=(n_seq,),
        in_specs=[pl.BlockSpec((None, Q_PAD, BRANCH_DIM), lambda b: (b, 0, 0)),
                  pl.BlockSpec((None, n_blk, BRANCH_DIM), lambda b: (b, 0, 0))],
        out_specs=pl.BlockSpec((None, Q_PAD, LANES), lambda b: (b, 0, 0)),
        out_shape=jax.ShapeDtypeStruct((n_seq, Q_PAD, LANES), I32),
        compiler_params=_params(1),
        name="moba_select",
    )(q_pad, k_mean.reshape(n_seq, n_blk, BRANCH_DIM))


def _moba_sample_kernel(ids_ref, pt_ref, q_ref, kn_ref, vn_ref, *refs, past_len):
    n_pg = 2 * MOBA_TOPK
    k_refs, v_refs, o_ref = refs[:n_pg], refs[n_pg:2 * n_pg], refs[2 * n_pg]
    qi = pl.program_id(2)
    qb = q_ref[...].astype(BF16)
    qpos = past_len + lax.broadcasted_iota(I32, (Q_PAD, 1), 0)
    krel = lax.broadcasted_iota(I32, (1, PAGE_SIZE), 1)
    own_mask = (past_len + krel) <= qpos
    logits = [_dot_nt(qb, r[...].astype(BF16)) * ATT_SCALE for r in k_refs]
    own = jnp.where(own_mask, _dot_nt(qb, kn_ref[...].astype(BF16)) * ATT_SCALE, NEG_BIG)
    top = jnp.max(own, axis=-1, keepdims=True)
    for lg in logits:
        top = jnp.maximum(top, jnp.max(lg, axis=-1, keepdims=True))
    p_own = jnp.where(own_mask, jnp.exp(own - top), 0.0)
    den = jnp.sum(p_own, axis=-1, keepdims=True)
    acc = _dot(p_own.astype(BF16), vn_ref[...].astype(BF16))
    for lg, vr in zip(logits, v_refs):
        p = jnp.exp(lg - top)
        den = den + jnp.sum(p, axis=-1, keepdims=True)
        acc = acc + _dot(p.astype(BF16), vr[...].astype(BF16))
    out = acc / den
    row = lax.broadcasted_iota(I32, (Q_PAD, 1), 0)
    o_ref[...] = jnp.sum(jnp.where(row == qi, out, 0.0), axis=0, keepdims=True).astype(o_ref.dtype)


def moba_sample(ids, page_table, q_pad, k_new, v_new, k_cache, v_cache, layer, *, n_q, past_len):
    n_seq = q_pad.shape[0]
    kc = k_cache.reshape(k_cache.shape[0], k_cache.shape[1], PAGE_SIZE, BRANCH_DIM)
    vc = v_cache.reshape(kc.shape)

    def page_map(b, h, q, ids_ref, pt_ref, *, slot, half):
        blk = ids_ref[((b * n_q + q) * N_HEADS + h) * MOBA_TOPK + slot]
        return (layer, pt_ref[b, 2 * blk + half], 0, h)

    page_specs = [pl.BlockSpec((None, None, PAGE_SIZE, HEAD_DIM), functools.partial(page_map, slot=s, half=hf))
                  for s in range(MOBA_TOPK) for hf in range(2)]
    head = lambda rows: pl.BlockSpec((None, rows, HEAD_DIM), lambda b, h, q, i_, p_: (b, 0, h))
    out = pl.pallas_call(
        functools.partial(_moba_sample_kernel, past_len=past_len),
        grid_spec=pltpu.PrefetchScalarGridSpec(
            num_scalar_prefetch=2,
            grid=(n_seq, N_HEADS, n_q),
            in_specs=[head(Q_PAD), head(PAGE_SIZE), head(PAGE_SIZE)] + page_specs + page_specs,
            out_specs=pl.BlockSpec((None, None, 1, HEAD_DIM), lambda b, h, q, i_, p_: (b, q, 0, h))),
        out_shape=jax.ShapeDtypeStruct((n_seq, n_q, 1, BRANCH_DIM), BF16),
        compiler_params=_params(3),
        name="moba_sample",
    )(ids, page_table, q_pad, k_new, v_new, *([kc] * (2 * MOBA_TOPK)), *([vc] * (2 * MOBA_TOPK)))
    return out.reshape(n_seq * n_q, BRANCH_DIM)


def _sb_sample_kernel(pt_ref, q_ref, kn_ref, vn_ref, *refs, past_len):
    pps = PAGES_PER_STEP
    k_refs, v_refs = refs[:pps], refs[pps:2 * pps]
    o_ref, carry_ref, acc_ref, live_ref = refs[2 * pps:]
    step = pl.program_id(1)
    n_steps = pl.num_programs(1)
    qpos = past_len + lax.broadcasted_iota(I32, (Q_PAD, 1), 0)
    krel = lax.broadcasted_iota(I32, (1, PAGE_SIZE), 1)
    tri = _strict_lower(PAGE_SIZE)

    def visit(k_ref, v_ref, k0):
        causal = (k0 + krel) < qpos
        live = jnp.float32(SB_EXIT)
        for h in range(N_HEADS):
            cols = slice(h * HEAD_DIM, (h + 1) * HEAD_DIM)
            carry, acc = _sb_block(q_ref[:, cols].astype(BF16), k_ref[:, cols].astype(BF16),
                                   v_ref[:, cols].astype(BF16), tri, causal,
                                   carry_ref[:, h * HEAD_DIM:h * HEAD_DIM + 1], acc_ref[:, cols])
            carry_ref[:, cols] = jnp.broadcast_to(carry, (Q_PAD, HEAD_DIM))
            acc_ref[:, cols] = acc
            live = jnp.maximum(live, jnp.max(carry))
        live_ref[0] = (live > SB_EXIT).astype(I32)

    @pl.when(step == 0)
    def _():
        carry_ref[...] = jnp.zeros_like(carry_ref)
        acc_ref[...] = jnp.zeros_like(acc_ref)
        visit(kn_ref, vn_ref, past_len)

    for k in range(pps):
        @pl.when((step > 0) & (live_ref[0] > 0))
        def _(k=k):
            page = past_len // PAGE_SIZE - 1 - ((step - 1) * pps + k)
            visit(k_refs[k], v_refs[k], page * PAGE_SIZE)

    @pl.when(step == n_steps - 1)
    def _():
        o_ref[...] = acc_ref[...].astype(o_ref.dtype)


def sb_sample(page_table, q_pad, k_new, v_new, k_cache, v_cache, layer, *, past_len):
    n_seq, n_pages = page_table.shape
    kc = k_cache.reshape(k_cache.shape[0], k_cache.shape[1], PAGE_SIZE, BRANCH_DIM)
    vc = v_cache.reshape(kc.shape)
    n_steps = n_pages // PAGES_PER_STEP + 1

    def page_map(b, s, pt_ref, *, k):
        page = n_pages - 1 - (jnp.maximum(s - 1, 0) * PAGES_PER_STEP + k)
        return (layer, pt_ref[b, page], 0, 0)

    page_specs = [pl.BlockSpec((None, None, PAGE_SIZE, BRANCH_DIM), functools.partial(page_map, k=k))
                  for k in range(PAGES_PER_STEP)]
    whole = lambda rows: pl.BlockSpec((None, rows, BRANCH_DIM), lambda b, s, pt: (b, 0, 0))
    return pl.pallas_call(
        functools.partial(_sb_sample_kernel, past_len=past_len),
        grid_spec=pltpu.PrefetchScalarGridSpec(
            num_scalar_prefetch=1,
            grid=(n_seq, n_steps),
            in_specs=[whole(Q_PAD), whole(PAGE_SIZE), whole(PAGE_SIZE)] + page_specs + page_specs,
            out_specs=whole(Q_PAD),
            scratch_shapes=[pltpu.VMEM((Q_PAD, BRANCH_DIM), F32), pltpu.VMEM((Q_PAD, BRANCH_DIM), F32),
                            pltpu.SMEM((1,), I32)]),
        out_shape=jax.ShapeDtypeStruct((n_seq, Q_PAD, BRANCH_DIM), BF16),
        compiler_params=_params(2),
        name="sb_sample",
    )(page_table, q_pad, k_new, v_new, *([kc] * PAGES_PER_STEP), *([vc] * PAGES_PER_STEP))


def _gate_mix_kernel(u_ref, ha_ref, hb_ref, hc_ref, hd_ref, *refs):
    wg_refs, wp_refs = refs[:N_BRANCH], refs[N_BRANCH:2 * N_BRANCH]
    o_ref, wgb_ref, wpb_ref = refs[2 * N_BRANCH:]

    @pl.when(pl.program_id(1) == 0)
    def _():
        for b in range(N_BRANCH):
            wgb_ref[b] = wg_refs[b][...].astype(BF16)
            wpb_ref[b] = wp_refs[b][...].astype(BF16)

    u = u_ref[...]
    acc = None
    for b, h_ref in enumerate((ha_ref, hb_ref, hc_ref, hd_ref)):
        term = jax.nn.sigmoid(_dot(u, wgb_ref[b])) * _dot(h_ref[...], wpb_ref[b])
        acc = term if acc is None else acc + term
    o_ref[...] = acc.astype(o_ref.dtype)


def gate_mix(u, hs, w_in, projs, layer):
    m = u.shape[0]
    tm, tn = min(m, 512), 256
    gate_specs = [pl.BlockSpec((None, D_MODEL, tn),
                               functools.partial(lambda j, i, b: (layer, 0, (N_PROJ + b * D_MODEL) // tn + j), b=b))
                  for b in range(N_BRANCH)]
    proj_specs = [pl.BlockSpec((None, BRANCH_DIM, tn), lambda j, i: (layer, 0, j))] * N_BRANCH
    h_spec = pl.BlockSpec((tm, BRANCH_DIM), lambda j, i: (i, 0))
    return pl.pallas_call(
        _gate_mix_kernel,
        grid=(D_MODEL // tn, m // tm),
        in_specs=[pl.BlockSpec((tm, D_MODEL), lambda j, i: (i, 0))] + [h_spec] * N_BRANCH + gate_specs + proj_specs,
        out_specs=pl.BlockSpec((tm, tn), lambda j, i: (i, j)),
        out_shape=jax.ShapeDtypeStruct((m, D_MODEL), BF16),
        scratch_shapes=[pltpu.VMEM((N_BRANCH, D_MODEL, tn), BF16), pltpu.VMEM((N_BRANCH, BRANCH_DIM, tn), BF16)],
        compiler_params=_params(2),
        name="gate_mix",
    )(u, *hs, *([w_in] * N_BRANCH), *projs)


def _new_weights(te_ref, i):
    prev = te_ref[jnp.maximum(i - 1, 0)]
    return (i == 0) | (te_ref[i] != prev)


def _ffn_up_kernel(te_ref, nv_ref, xa_ref, xb_ref, w1_ref, w3_ref, o_ref, w1b_ref, w3b_ref):
    i = pl.program_id(1)

    @pl.when(_new_weights(te_ref, i))
    def _():
        w1b_ref[...] = w1_ref[...].astype(BF16)
        w3b_ref[...] = w3_ref[...].astype(BF16)

    @pl.when(i < nv_ref[0])
    def _():
        half = xa_ref.shape[1]
        xa, xb = xa_ref[...], xb_ref[...]
        a = _dot(xa, w1b_ref[:half]) + _dot(xb, w1b_ref[half:])
        g = _dot(xa, w3b_ref[:half]) + _dot(xb, w3b_ref[half:])
        o_ref[...] = (a * jax.nn.sigmoid(a) * g).astype(o_ref.dtype)

    @pl.when(i >= nv_ref[0])
    def _():
        o_ref[...] = jnp.zeros_like(o_ref)


def ffn_up(xa, xb, w1, w3, tile_expert, n_valid, tm, cols=(0, 0)):
    r = xa.shape[0]
    half = D_MODEL // 2
    tn = 512
    row = lambda c: (lambda j, i, te, nv: (jnp.minimum(i, nv[0] - 1), c))
    wmap = lambda j, i, te, nv: (te[i], 0, j)
    return pl.pallas_call(
        _ffn_up_kernel,
        grid_spec=pltpu.PrefetchScalarGridSpec(
            num_scalar_prefetch=2,
            grid=(D_FF // tn, r // tm),
            in_specs=[pl.BlockSpec((tm, half), row(cols[0])), pl.BlockSpec((tm, half), row(cols[1])),
                      pl.BlockSpec((None, D_MODEL, tn), wmap), pl.BlockSpec((None, D_MODEL, tn), wmap)],
            out_specs=pl.BlockSpec((tm, tn), lambda j, i, te, nv: (i, j)),
            scratch_shapes=[pltpu.VMEM((D_MODEL, tn), BF16), pltpu.VMEM((D_MODEL, tn), BF16)]),
        out_shape=jax.ShapeDtypeStruct((r, D_FF), BF16),
        compiler_params=_params(2),
        name="ffn_up",
    )(tile_expert, n_valid, xa, xb, w1, w3)


def _ffn_down_kernel(te_ref, nv_ref, h_ref, w_ref, *refs, gated):
    i = pl.program_id(1)
    if gated:
        g_ref, o_ref, wb_ref = refs
    else:
        o_ref, wb_ref = refs

    @pl.when(_new_weights(te_ref, i))
    def _():
        wb_ref[...] = w_ref[...].astype(BF16)

    @pl.when(i < nv_ref[0])
    def _():
        y = _dot(h_ref[...], wb_ref[...])
        o_ref[...] = y * g_ref[...] if gated else y

    @pl.when(i >= nv_ref[0])
    def _():
        o_ref[...] = jnp.zeros_like(o_ref)


def ffn_down(h, w2, tile_expert, n_valid, tm, gate=None):
    r = h.shape[0]
    tn = 256
    ins = [tile_expert, n_valid, h, w2]
    in_specs = [pl.BlockSpec((tm, D_FF), lambda j, i, te, nv: (jnp.minimum(i, nv[0] - 1), 0)),
                pl.BlockSpec((None, D_FF, tn), lambda j, i, te, nv: (te[i], 0, j))]
    if gate is not None:
        ins.append(gate)
        in_specs.append(pl.BlockSpec((tm, 1), lambda j, i, te, nv: (i, 0)))
    return pl.pallas_call(
        functools.partial(_ffn_down_kernel, gated=gate is not None),
        grid_spec=pltpu.PrefetchScalarGridSpec(
            num_scalar_prefetch=2,
            grid=(D_MODEL // tn, r // tm),
            in_specs=in_specs,
            out_specs=pl.BlockSpec((tm, tn), lambda j, i, te, nv: (i, j)),
            scratch_shapes=[pltpu.VMEM((D_FF, tn), BF16)]),
        out_shape=jax.ShapeDtypeStruct((r, D_MODEL), F32),
        compiler_params=_params(2),
        name="ffn_down",
    )(*ins)


def dense_ffn(u, w1, w3, w2, j):
    m = u.shape[0]
    tm = min(m, 512)
    te = jnp.full((m // tm,), j, I32)
    nv = jnp.full((1,), m // tm, I32)
    h = ffn_up(u, u, w1, w3, te, nv, tm, cols=(0, 1))
    return ffn_down(h, w2, te, nv, tm)


MOE_TM = 256


def _moe_gather_kernel(src_ref, x_ref, xa_ref, xb_ref, buf_ref, *, tm):
    base = pl.program_id(1) * tm

    def body(r, _):
        buf_ref[pl.ds(r, 1), :] = x_ref[pl.ds(src_ref[base + r], 1), :]
        return 0

    lax.fori_loop(0, tm, body, 0, unroll=8)
    w = buf_ref[...]
    xa_ref[...] = pltpu.bitcast(w & jnp.uint32(0xFFFF0000), F32).astype(BF16)
    xb_ref[...] = pltpu.bitcast(w << 16, F32).astype(BF16)


def moe_gather(packed, src_tok, n_rows):
    t, half = packed.shape
    tn = half // 2
    return pl.pallas_call(
        functools.partial(_moe_gather_kernel, tm=MOE_TM),
        grid_spec=pltpu.PrefetchScalarGridSpec(
            num_scalar_prefetch=1,
            grid=(half // tn, n_rows // MOE_TM),
            in_specs=[pl.BlockSpec((t, tn), lambda c, g, src: (0, c))],
            out_specs=(pl.BlockSpec((MOE_TM, tn), lambda c, g, src: (g, c)),
                       pl.BlockSpec((MOE_TM, tn), lambda c, g, src: (g, c))),
            scratch_shapes=[pltpu.VMEM((MOE_TM, tn), jnp.uint32)]),
        out_shape=(jax.ShapeDtypeStruct((n_rows, half), BF16), jax.ShapeDtypeStruct((n_rows, half), BF16)),
        compiler_params=_params(2),
        name="moe_gather",
    )(src_tok, packed)


def _moe_combine_kernel(p0_ref, p1_ref, y_ref, o_ref, *, tm):
    base = pl.program_id(1) * tm

    def body(r, _):
        o_ref[pl.ds(r, 1), :] = y_ref[pl.ds(p0_ref[base + r], 1), :] + y_ref[pl.ds(p1_ref[base + r], 1), :]
        return 0

    lax.fori_loop(0, tm, body, 0, unroll=8)


def moe_combine(y_sorted, pos0, pos1, n_tok, tm):
    r = y_sorted.shape[0]
    tn = 256
    return pl.pallas_call(
        functools.partial(_moe_combine_kernel, tm=tm),
        grid_spec=pltpu.PrefetchScalarGridSpec(
            num_scalar_prefetch=2,
            grid=(D_MODEL // tn, n_tok // tm),
            in_specs=[pl.BlockSpec((r, tn), lambda c, i, p0, p1: (0, c))],
            out_specs=pl.BlockSpec((tm, tn), lambda c, i, p0, p1: (i, c))),
        out_shape=jax.ShapeDtypeStruct((n_tok, D_MODEL), F32),
        compiler_params=_params(2),
        name="moe_combine",
    )(pos0, pos1, y_sorted)


def moe_ffn(packed, route, w1, w3, w2, j):
    t = packed.shape[0]
    n_pairs = 2 * t
    tm = MOE_TM
    n_tiles = -(-(n_pairs + N_EXPERTS * (tm - 1)) // tm)
    n_rows = n_tiles * tm
    expert = route[:, :2].astype(I32).reshape(n_pairs)
    gate = route[:, 2:4].reshape(n_pairs)
    onehot = (expert[:, None] == jnp.arange(N_EXPERTS, dtype=I32)[None, :]).astype(I32)
    running = jnp.cumsum(onehot, axis=0)
    counts = running[-1]
    rank = jnp.take_along_axis(running, expert[:, None], axis=1)[:, 0] - 1
    padded = ((counts + tm - 1) // tm) * tm
    ends = jnp.cumsum(padded)
    starts = ends - padded
    dest = starts[expert] + rank
    src_tok = jnp.zeros((n_rows,), I32).at[dest].set(jnp.arange(n_pairs, dtype=I32) // 2)
    gate_rows = jnp.zeros((n_rows,), F32).at[dest].set(gate)
    n_valid = (ends[-1] // tm).astype(I32).reshape(1)
    tile_start = jnp.arange(n_tiles, dtype=I32) * tm
    tile_expert = jnp.minimum(jnp.searchsorted(ends, tile_start, side="right"), N_EXPERTS - 1).astype(I32)
    last_expert = tile_expert[jnp.maximum(n_valid[0] - 1, 0)]
    tile_expert = jnp.where(tile_start < ends[-1], tile_expert, last_expert) + j * N_EXPERTS
    flat = lambda w: w.reshape((-1,) + w.shape[2:])

    xa, xb = moe_gather(packed, src_tok, n_rows)
    h = ffn_up(xa, xb, flat(w1), flat(w3), tile_expert, n_valid, tm)
    y = ffn_down(h, flat(w2), tile_expert, n_valid, tm, gate=gate_rows.reshape(n_rows, 1))
    t_pad = -(-t // tm) * tm
    pos = jnp.zeros((t_pad, 2), I32).at[:t].set(dest.reshape(t, 2))
    return moe_combine(y, pos[:, 0], pos[:, 1], t_pad, tm)[:t]


def kernel(x_prompt, x_sample, cache_moba_k, cache_moba_v, cache_sb_k, cache_sb_v, state_pool, state_conv,
           page_table, c_prompt, c_sample, w_ada, b_ada, w_in, pool_w, pool_scale, pool_proj, moba_proj,
           conv_w, conv_b, conv_ln_g, conv_ln_b, conv_pw, sb_proj, w_out, ln1_g, ln1_b, ln2_g, ln2_b,
           ffn_w1, ffn_w3, ffn_w2, moe_router, moe_router_b, moe_w1, moe_w3, moe_w2):
    bp, lp, _ = x_prompt.shape
    bs, ls, _ = x_sample.shape
    mp, ms = bp * lp, bs * ls
    past_len = page_table.shape[1] * PAGE_SIZE
    assert past_len // MOBA_BLOCK >= MOBA_TOPK and ls <= Q_PAD and lp % MOBA_BLOCK == 0

    c_all = jnp.zeros((16, D_MODEL), F32).at[:bp].set(c_prompt).at[bp:bp + bs].set(c_sample)
    mod = ada_mod(c_all, w_ada, b_ada).reshape(DEPTH, 16, 6, D_MODEL)

    def mods(l, k):
        mp_ = mod[l, :bp, k].reshape(bp, 1, D_MODEL)
        ms_ = jnp.repeat(mod[l, bp:bp + bs, k], ls, axis=0).reshape(1, ms, D_MODEL)
        return mp_, ms_

    pos_p = jnp.arange(lp, dtype=I32)
    pos_s = past_len + jnp.arange(ls, dtype=I32)
    tab_p, tab_s = rope_tables(pos_p), rope_tables(pos_s)

    xp = x_prompt.reshape(mp, D_MODEL)
    xs = x_sample.reshape(ms, D_MODEL)
    sc1, sh1 = mods(0, 1), mods(0, 0)
    up = ln_mod(xp, sc1[0], sh1[0], lp)
    us = ln_mod(xs, sc1[1], sh1[1], ls)
    zero_pool = jnp.zeros((bp, POOL_HALO, BRANCH_DIM), F32)
    zero_conv = jnp.zeros((bp, CONV_HALO, BRANCH_DIM), F32)
    outs_p = [[] for _ in range(6)]
    outs_s = [[] for _ in range(6)]

    def head4(t, b, l):
        return t.reshape(b, l, N_HEADS, HEAD_DIM)

    for l in range(DEPTH):
        g1p, g1s = mods(l, 2)
        sc2, sh2 = mods(l, 4), mods(l, 3)
        g2p, g2s = mods(l, 5)
        projs = (pool_proj, moba_proj, conv_pw, sb_proj)

        p3 = matmul(up, w_in, l, n=N_PROJ).reshape(bp, lp, N_PROJ)
        ha = pool_branch(p3, zero_pool, pool_w[l], pool_scale[l], pos0=0)
        hc, glu = conv_branch(p3, zero_conv, conv_w[l], conv_b[l], conv_ln_g[l], conv_ln_b[l])
        q_rot, k_rot, k_mean = rope_qk(p3, tab_p, with_means=True)
        hb = moba_prompt(q_rot, k_rot, p3, k_mean)
        hd = sb_prompt(p3)
        flat_p = lambda t: t.reshape(mp, BRANCH_DIM)
        mix = gate_mix(up, (flat_p(ha), flat_p(hb), flat_p(hc), flat_p(hd)), w_in, projs, l)
        attn = matmul(mix, w_out, l)
        outs_p[0].append(head4(k_rot, bp, lp))
        outs_p[1].append(head4(p3[:, :, 3 * BRANCH_DIM:4 * BRANCH_DIM], bp, lp))
        outs_p[2].append(head4(p3[:, :, 7 * BRANCH_DIM:8 * BRANCH_DIM], bp, lp))
        outs_p[3].append(head4(p3[:, :, 8 * BRANCH_DIM:9 * BRANCH_DIM], bp, lp))
        outs_p[4].append(p3[:, lp - POOL_BUF:, :BRANCH_DIM])
        outs_p[5].append(glu[:, lp - CONV_BUF:])

        ps3 = matmul(us, w_in, l, n=N_PROJ).reshape(bs, ls, N_PROJ)
        pool_hist = jnp.concatenate([state_pool[l], ps3[:, :, :BRANCH_DIM]], axis=1)
        pool_init = jnp.pad(state_pool[l], ((0, 0), (POOL_HALO - POOL_BUF, 0), (0, 0)))
        has = pool_branch(ps3, pool_init, pool_w[l], pool_scale[l], pos0=past_len)
        conv_init = jnp.pad(state_conv[l], ((0, 0), (CONV_HALO - CONV_BUF, 0), (0, 0)))
        hcs, glu_s = conv_branch(ps3, conv_init, conv_w[l], conv_b[l], conv_ln_g[l], conv_ln_b[l])
        conv_hist = jnp.concatenate([state_conv[l], glu_s], axis=1)
        qs_rot, ks_rot = rope_qk(ps3, tab_s, with_means=False)
        pad_q = lambda t: jnp.pad(t, ((0, 0), (0, Q_PAD - ls), (0, 0)))
        pad_kv = lambda t: jnp.pad(t, ((0, 0), (0, PAGE_SIZE - ls), (0, 0)))
        mv_s = ps3[:, :, 3 * BRANCH_DIM:4 * BRANCH_DIM]
        sq_s, sk_s, sv_s = (ps3[:, :, k * BRANCH_DIM:(k + 1) * BRANCH_DIM] for k in (6, 7, 8))
        k_mean_s = page_block_means(cache_moba_k, l, page_table)
        picks = moba_select(pad_q(qs_rot), k_mean_s)
        ids = picks[:, :ls, :N_HEADS * MOBA_TOPK].reshape(-1)
        hbs = moba_sample(ids, page_table, pad_q(qs_rot), pad_kv(ks_rot), pad_kv(mv_s),
                          cache_moba_k, cache_moba_v, l, n_q=ls, past_len=past_len)
        hds = sb_sample(page_table, pad_q(sq_s), pad_kv(sk_s), pad_kv(sv_s), cache_sb_k, cache_sb_v, l,
                        past_len=past_len)[:, :ls].reshape(ms, BRANCH_DIM)
        flat_s = lambda t: t.reshape(ms, BRANCH_DIM)
        mix_s = gate_mix(us, (flat_s(has), hbs, flat_s(hcs), hds), w_in, projs, l)
        attn_s = matmul(mix_s, w_out, l)
        outs_s[0].append(head4(ks_rot, bs, ls))
        outs_s[1].append(head4(mv_s, bs, ls))
        outs_s[2].append(head4(sk_s, bs, ls))
        outs_s[3].append(head4(sv_s, bs, ls))
        outs_s[4].append(pool_hist[:, -POOL_BUF:])
        outs_s[5].append(conv_hist[:, -CONV_BUF:])

        j = l // 2
        if l % 2 == 0:
            xp, u2p = resid_ln(xp, attn, g1p, ln1_g[l], ln1_b[l], lp, sc2[0], sh2[0])
            xs, u2s = resid_ln(xs, attn_s, g1s, ln1_g[l], ln1_b[l], ls, sc2[1], sh2[1])
            fp = dense_ffn(u2p, ffn_w1, ffn_w3, ffn_w2, j)
            fs = dense_ffn(u2s, ffn_w1, ffn_w3, ffn_w2, j)
        else:
            router = (moe_router[j], moe_router_b[j])
            xp, pk_p, rt_p = resid_ln(xp, attn, g1p, ln1_g[l], ln1_b[l], lp, sc2[0], sh2[0], router)
            xs, pk_s, rt_s = resid_ln(xs, attn_s, g1s, ln1_g[l], ln1_b[l], ls, sc2[1], sh2[1], router)
            f_all = moe_ffn(jnp.concatenate([pk_p, pk_s], axis=0), jnp.concatenate([rt_p, rt_s], axis=0),
                            moe_w1, moe_w3, moe_w2, j)
            fp, fs = f_all[:mp], f_all[mp:]
        if l + 1 < DEPTH:
            nsc, nsh = mods(l + 1, 1), mods(l + 1, 0)
            xp, up = resid_ln(xp, fp, g2p, ln2_g[l], ln2_b[l], lp, nsc[0], nsh[0])
            xs, us = resid_ln(xs, fs, g2s, ln2_g[l], ln2_b[l], ls, nsc[1], nsh[1])
        else:
            xp = resid_ln(xp, fp, g2p, ln2_g[l], ln2_b[l], lp)
            xs = resid_ln(xs, fs, g2s, ln2_g[l], ln2_b[l], ls)

    stack = lambda group: tuple(jnp.stack(t) for t in group)
    return (xp.reshape(bp, lp, D_MODEL), xs.reshape(bs, ls, D_MODEL)) + stack(outs_p) + stack(outs_s)
```

```python
import functools
import math

import jax
import jax.numpy as jnp
from jax import lax
from jax.experimental import pallas as pl
from jax.experimental.pallas import tpu as pltpu

F32 = jnp.float32
BF16 = jnp.bfloat16
I32 = jnp.int32

D_MODEL = 2048
DEPTH = 4
PAGE_SIZE = 128
HEAD_DIM = 128
BRANCH_DIM = 512
N_HEADS = BRANCH_DIM // HEAD_DIM
POOL_WINDOWS = (2, 4, 8, 16)
POOL_BUF = 15
POOL_HALO = 16
MOBA_BLOCK = 256
MOBA_TOPK = 3
ROPE_THETA = 500000.0
ROPE_DIM = HEAD_DIM // 4
CONV_WIDTH = 31
CONV_BUF = CONV_WIDTH - 1
CONV_HALO = 32
N_BRANCH = 4
D_FF = 5632
N_EXPERTS = 8
ALPHA = (2.0 * DEPTH) ** 0.25
LN_EPS = 1e-5
N_PROJ = 9 * BRANCH_DIM
ATT_SCALE = HEAD_DIM ** -0.5

SB_EXIT = -104.0
NEG_BIG = -1e30

VMEM_LIMIT_V7X = 56 * 1024 * 1024
LANES = 128


def _params(n_axes):
    return pltpu.CompilerParams(dimension_semantics=("arbitrary",) * n_axes,
                                vmem_limit_bytes=VMEM_LIMIT_V7X)


def _ln(x):
    mu = jnp.mean(x, axis=-1, keepdims=True)
    xc = x - mu
    var = jnp.mean(xc * xc, axis=-1, keepdims=True)
    return xc * lax.rsqrt(var + LN_EPS)


def _dot(a, b):
    return jnp.dot(a, b, preferred_element_type=F32)


def _dot_nt(a, b):
    return lax.dot_general(a, b, (((1,), (1,)), ((), ())), preferred_element_type=F32)


def _split2(x):
    hi = x.astype(BF16)
    lo = (x - hi.astype(F32)).astype(BF16)
    return hi, lo


def _dot_nt_precise(a, b):
    ah, al = _split2(a)
    bh, bl = _split2(b)
    return _dot_nt(ah, bh) + (_dot_nt(ah, bl) + _dot_nt(al, bh))


def _dot_exact_rhs(a, b_bf16):
    hi, lo = _split2(a)
    return _dot(hi, b_bf16) + _dot(lo, b_bf16)


def _softplus_neg_abs(z):
    return jnp.log1p(jnp.exp(-jnp.abs(z)))


def _ada_kernel(c_ref, w_ref, b_ref, o_ref):
    c = c_ref[...]
    s = (c * jax.nn.sigmoid(c)).astype(BF16)
    o_ref[...] = _dot(s, w_ref[...].astype(BF16)) + b_ref[...]


def ada_mod(c_all, w_ada, b_ada):
    rows = c_all.shape[0]
    tn = 1024
    n = w_ada.shape[-1]
    return pl.pallas_call(
        _ada_kernel,
        grid=(DEPTH, n // tn),
        in_specs=[pl.BlockSpec((rows, D_MODEL), lambda l, j: (0, 0)),
                  pl.BlockSpec((None, D_MODEL, tn), lambda l, j: (l, 0, j)),
                  pl.BlockSpec((None, 1, tn), lambda l, j: (l, 0, j))],
        out_specs=pl.BlockSpec((None, rows, tn), lambda l, j: (l, 0, j)),
        out_shape=jax.ShapeDtypeStruct((DEPTH, rows, n), F32),
        compiler_params=_params(2),
        name="ada_mod",
    )(c_all, w_ada, b_ada.reshape(DEPTH, 1, n))


def _mod_spec(mod, tm, tiles_per_group):
    if mod.shape[1] == 1:
        return pl.BlockSpec((None, 1, D_MODEL), lambda i: (i // tiles_per_group, 0, 0))
    return pl.BlockSpec((None, tm, D_MODEL), lambda i: (0, i, 0))


def _ln_mod_kernel(x_ref, sc_ref, sh_ref, o_ref):
    o_ref[...] = (_ln(x_ref[...]) * (1.0 + sc_ref[...]) + sh_ref[...]).astype(o_ref.dtype)


def ln_mod(x, sc, sh, rows_per_group):
    m = x.shape[0]
    tm = min(m, 256)
    tpg = max(rows_per_group // tm, 1)
    row = pl.BlockSpec((tm, D_MODEL), lambda i: (i, 0))
    return pl.pallas_call(
        _ln_mod_kernel,
        grid=(m // tm,),
        in_specs=[row, _mod_spec(sc, tm, tpg), _mod_spec(sh, tm, tpg)],
        out_specs=row,
        out_shape=jax.ShapeDtypeStruct((m, D_MODEL), BF16),
        compiler_params=_params(1),
        name="ln_mod",
    )(x, sc, sh)


def _pack_bf16_pairs(u):
    half = u.shape[1] // 2
    a = pltpu.bitcast(u[:, :half].astype(BF16).astype(F32), jnp.uint32)
    b = pltpu.bitcast(u[:, half:].astype(BF16).astype(F32), jnp.uint32)
    return (a & jnp.uint32(0xFFFF0000)) | (b >> 16)


def _route_top2(u, rh_ref, rl_ref, rb_ref):
    uh, ul = _split2(u)
    lg = _dot(uh, rh_ref[...]) + (_dot(uh, rl_ref[...]) + _dot(ul, rh_ref[...])) + rb_ref[...]
    lane = lax.broadcasted_iota(I32, lg.shape, 1)
    m1 = jnp.max(lg, axis=-1, keepdims=True)
    i1 = jnp.min(jnp.where(lg == m1, lane, LANES), axis=-1, keepdims=True)
    lg2 = jnp.where(lane == i1, -jnp.inf, lg)
    m2 = jnp.max(lg2, axis=-1, keepdims=True)
    i2 = jnp.min(jnp.where(lg2 == m2, lane, LANES), axis=-1, keepdims=True)
    e2 = jnp.exp(m2 - m1)
    g1 = 1.0 / (1.0 + e2)
    g2 = e2 / (1.0 + e2)
    out = jnp.where(lane == 0, i1.astype(F32), 0.0)
    out = jnp.where(lane == 1, i2.astype(F32), out)
    out = jnp.where(lane == 2, g1, out)
    out = jnp.where(lane == 3, g2, out)
    return out


def _resid_ln_kernel(*refs, mode):
    x_ref, y_ref, g_ref, lg_ref, lb_ref = refs[:5]
    t = ALPHA * x_ref[...] + g_ref[...] * y_ref[...]
    xn = _ln(t) * lg_ref[...] + lb_ref[...]
    if mode == "last":
        refs[5][...] = xn
        return
    sc_ref, sh_ref = refs[5:7]
    u = _ln(xn) * (1.0 + sc_ref[...]) + sh_ref[...]
    if mode == "next":
        xo_ref, uo_ref = refs[7:9]
        xo_ref[...] = xn
        uo_ref[...] = u.astype(BF16)
    else:
        rh_ref, rl_ref, rb_ref, xo_ref, up_ref, rt_ref = refs[7:13]
        xo_ref[...] = xn
        up_ref[...] = _pack_bf16_pairs(u)
        rt_ref[...] = _route_top2(u, rh_ref, rl_ref, rb_ref)


def resid_ln(x, y, g, ln_g, ln_b, rows_per_group, sc=None, sh=None, router=None, y_row0=0):
    m = x.shape[0]
    tm = min(m, 256)
    tpg = max(rows_per_group // tm, 1)
    assert y_row0 % tm == 0
    row = pl.BlockSpec((tm, D_MODEL), lambda i: (i, 0))
    vec = pl.BlockSpec((1, D_MODEL), lambda i: (0, 0))
    ins = [x, y, g, ln_g.reshape(1, D_MODEL), ln_b.reshape(1, D_MODEL)]
    in_specs = [row, pl.BlockSpec((tm, D_MODEL), lambda i: (y_row0 // tm + i, 0)), _mod_spec(g, tm, tpg), vec, vec]
    x_out = jax.ShapeDtypeStruct((m, D_MODEL), F32)
    if sc is None:
        mode, out_shape, out_specs = "last", x_out, row
    else:
        ins += [sc, sh]
        in_specs += [_mod_spec(sc, tm, tpg), _mod_spec(sh, tm, tpg)]
        if router is None:
            mode = "next"
            out_shape = (x_out, jax.ShapeDtypeStruct((m, D_MODEL), BF16))
            out_specs = (row, row)
        else:
            mode = "route"
            rw, rb = router
            rw_pad = jnp.zeros((D_MODEL, LANES), F32).at[:, :N_EXPERTS].set(rw)
            rhi = rw_pad.astype(BF16)
            rlo = (rw_pad - rhi.astype(F32)).astype(BF16)
            rb_pad = jnp.full((1, LANES), -jnp.inf, F32).at[0, :N_EXPERTS].set(rb)
            ins += [rhi, rlo, rb_pad]
            full = pl.BlockSpec((D_MODEL, LANES), lambda i: (0, 0))
            in_specs += [full, full, pl.BlockSpec((1, LANES), lambda i: (0, 0))]
            out_shape = (x_out, jax.ShapeDtypeStruct((m, D_MODEL // 2), jnp.uint32),
                         jax.ShapeDtypeStruct((m, LANES), F32))
            out_specs = (row, pl.BlockSpec((tm, D_MODEL // 2), lambda i: (i, 0)),
                         pl.BlockSpec((tm, LANES), lambda i: (i, 0)))
    return pl.pallas_call(
        functools.partial(_resid_ln_kernel, mode=mode),
        grid=(m // tm,),
        in_specs=in_specs, out_specs=out_specs, out_shape=out_shape,
        compiler_params=_params(1),
        name="resid_ln_" + mode,
    )(*ins)


def _mm_kernel(x_ref, w_ref, o_ref, wb_ref):
    @pl.when(pl.program_id(1) == 0)
    def _():
        wb_ref[...] = w_ref[...].astype(BF16)
    o_ref[...] = _dot(x_ref[...], wb_ref[...]).astype(o_ref.dtype)


def matmul(x, w, layer, *, col0=0, n=None, tn=512, tm=1024, out_dtype=F32):
    m, k = x.shape
    n = w.shape[-1] if n is None else n
    tm = min(m, tm)
    cb = col0 // tn
    return pl.pallas_call(
        _mm_kernel,
        grid=(n // tn, m // tm),
        in_specs=[pl.BlockSpec((tm, k), lambda j, i: (i, 0)),
                  pl.BlockSpec((None, k, tn), lambda j, i: (layer, 0, cb + j))],
        out_specs=pl.BlockSpec((tm, tn), lambda j, i: (i, j)),
        out_shape=jax.ShapeDtypeStruct((m, n), out_dtype),
        scratch_shapes=[pltpu.VMEM((k, tn), BF16)],
        compiler_params=_params(2),
        name="matmul",
    )(x, w)


def _pool_kernel(x_ref, init_ref, wg_ref, scale_ref, o_ref, halo_ref, ext_ref, *, tm, n_tiles, pos0):
    i = pl.program_id(1)

    @pl.when(i == 0)
    def _():
        halo_ref[...] = init_ref[...]

    x = x_ref[...]
    ext_ref[0:POOL_HALO, :] = halo_ref[...]
    ext_ref[POOL_HALO:POOL_HALO + tm, :] = x
    if n_tiles > 1:
        halo_ref[...] = x[tm - POOL_HALO:, :]
    pos = pos0 + i * tm + lax.broadcasted_iota(I32, (tm, 1), 0)
    for g, w in enumerate(POOL_WINDOWS):
        cols = pl.ds(g * LANES, LANES)
        s = ext_ref[pl.ds(POOL_HALO, tm), cols]
        for j in range(1, w):
            s = s + ext_ref[pl.ds(POOL_HALO - j, tm), cols]
        cnt = jnp.minimum(pos + 1, w).astype(F32)
        d = s / cnt - x[:, g * LANES:(g + 1) * LANES]
        y = _dot(d.astype(BF16), wg_ref[g].astype(BF16))
        o_ref[:, g * LANES:(g + 1) * LANES] = (y * scale_ref[:, g * LANES:(g + 1) * LANES]).astype(o_ref.dtype)


def pool_branch(proj, init, w_grp, scale, *, pos0):
    n_seq, seq_len, _ = proj.shape
    tm = min(seq_len, 256)
    n_tiles = seq_len // tm
    return pl.pallas_call(
        functools.partial(_pool_kernel, tm=tm, n_tiles=n_tiles, pos0=pos0),
        grid=(n_seq, n_tiles),
        in_specs=[pl.BlockSpec((None, tm, BRANCH_DIM), lambda b, i: (b, i, 0)),
                  pl.BlockSpec((None, POOL_HALO, BRANCH_DIM), lambda b, i: (b, 0, 0)),
                  pl.BlockSpec((len(POOL_WINDOWS), LANES, LANES), lambda b, i: (0, 0, 0)),
                  pl.BlockSpec((1, BRANCH_DIM), lambda b, i: (0, 0))],
        out_specs=pl.BlockSpec((None, tm, BRANCH_DIM), lambda b, i: (b, i, 0)),
        out_shape=jax.ShapeDtypeStruct((n_seq, seq_len, BRANCH_DIM), BF16),
        scratch_shapes=[pltpu.VMEM((POOL_HALO, BRANCH_DIM), F32),
                        pltpu.VMEM((POOL_HALO + tm, BRANCH_DIM), F32)],
        compiler_params=_params(2),
        name="pool_branch",
    )(proj, init, w_grp, scale.reshape(1, BRANCH_DIM))


def _conv_kernel(a_ref, init_ref, w_ref, b_ref, g_ref, beta_ref, o_ref, h_ref,
                 halo_ref, ext_ref, y_ref, *, tm, n_tiles, rc):
    i = pl.program_id(1)

    @pl.when(i == 0)
    def _():
        halo_ref[...] = init_ref[...]

    a = a_ref[...]
    h = a[:, :BRANCH_DIM] * jax.nn.sigmoid(a[:, BRANCH_DIM:])
    h_ref[...] = h
    ext_ref[0:CONV_HALO, :] = halo_ref[...]
    ext_ref[CONV_HALO:CONV_HALO + tm, :] = h
    if n_tiles > 1:
        halo_ref[...] = h[tm - CONV_HALO:, :]
    off = CONV_HALO - CONV_BUF
    for c in range(BRANCH_DIM // LANES):
        cols = pl.ds(c * LANES, LANES)
        for r0 in range(0, tm, rc):
            acc = jnp.zeros((rc, LANES), F32) + b_ref[:, c * LANES:(c + 1) * LANES]
            for k in range(CONV_WIDTH):
                acc = acc + ext_ref[pl.ds(r0 + off + k, rc), cols] * w_ref[k:k + 1, c * LANES:(c + 1) * LANES]
            y_ref[pl.ds(r0, rc), cols] = acc
    y = _ln(y_ref[...]) * g_ref[...] + beta_ref[...]
    o_ref[...] = (y * jax.nn.sigmoid(y)).astype(o_ref.dtype)


def conv_branch(proj, init, w, b, g, beta):
    n_seq, seq_len, _ = proj.shape
    tm = min(seq_len, 128)
    rc = min(tm, 64)
    n_tiles = seq_len // tm
    col_block = (4 * BRANCH_DIM) // (2 * BRANCH_DIM)
    w_pad = jnp.zeros((CONV_HALO, BRANCH_DIM), F32).at[:CONV_WIDTH].set(w)
    vec = pl.BlockSpec((1, BRANCH_DIM), lambda bb, i: (0, 0))
    row = pl.BlockSpec((None, tm, BRANCH_DIM), lambda bb, i: (bb, i, 0))
    return pl.pallas_call(
        functools.partial(_conv_kernel, tm=tm, n_tiles=n_tiles, rc=rc),
        grid=(n_seq, n_tiles),
        in_specs=[pl.BlockSpec((None, tm, 2 * BRANCH_DIM), lambda bb, i: (bb, i, col_block)),
                  pl.BlockSpec((None, CONV_HALO, BRANCH_DIM), lambda bb, i: (bb, 0, 0)),
                  pl.BlockSpec((CONV_HALO, BRANCH_DIM), lambda bb, i: (0, 0)),
                  vec, vec, vec],
        out_specs=(row, row),
        out_shape=(jax.ShapeDtypeStruct((n_seq, seq_len, BRANCH_DIM), BF16),
                   jax.ShapeDtypeStruct((n_seq, seq_len, BRANCH_DIM), F32)),
        scratch_shapes=[pltpu.VMEM((CONV_HALO, BRANCH_DIM), F32),
                        pltpu.VMEM((CONV_HALO + tm, BRANCH_DIM), F32),
                        pltpu.VMEM((tm, BRANCH_DIM), F32)],
        compiler_params=_params(2),
        name="conv_branch",
    )(proj, init, w_pad, b.reshape(1, -1), g.reshape(1, -1), beta.reshape(1, -1))


def rope_tables(pos):
    half = ROPE_DIM // 2
    inv_freq = ROPE_THETA ** (-jnp.arange(half, dtype=F32) / half)
    ang = pos.astype(F32)[:, None] * inv_freq[None, :]
    cos, sin = jnp.cos(ang), jnp.sin(ang)
    n = pos.shape[0]
    z = jnp.zeros((n, half), F32)
    rest = HEAD_DIM - ROPE_DIM
    cos_t = jnp.concatenate([cos, cos, jnp.ones((n, rest), F32)], axis=1)
    sin_up = jnp.concatenate([-sin, z, jnp.zeros((n, rest), F32)], axis=1)
    sin_dn = jnp.concatenate([z, sin, jnp.zeros((n, rest), F32)], axis=1)
    return cos_t, sin_up, sin_dn


def _rope(x, cos_t, sin_up, sin_dn):
    half = ROPE_DIM // 2
    width = x.shape[1]
    tile = lambda t: jnp.concatenate([t] * (width // HEAD_DIM), axis=1)
    up = pltpu.roll(x, width - half, axis=1)
    dn = pltpu.roll(x, half, axis=1)
    return x * tile(cos_t) + up * tile(sin_up) + dn * tile(sin_dn)


def _rope_kernel(q_ref, k_ref, cos_ref, su_ref, sd_ref, qo_ref, ko_ref, *maybe_km_ref, tm):
    cos_t, su, sd = cos_ref[...], su_ref[...], sd_ref[...]
    qo_ref[...] = _rope(q_ref[...], cos_t, su, sd)
    kr = _rope(k_ref[...], cos_t, su, sd)
    ko_ref[...] = kr
    if maybe_km_ref:
        maybe_km_ref[0][...] = jnp.sum(kr, axis=0, keepdims=True) * (1.0 / tm)


def rope_qk(proj, tables, *, with_means):
    n_seq, seq_len, _ = proj.shape
    tm = min(seq_len, MOBA_BLOCK)
    n_tiles = seq_len // tm
    row = pl.BlockSpec((None, tm, BRANCH_DIM), lambda b, i: (b, i, 0))
    tab = pl.BlockSpec((tm, HEAD_DIM), lambda b, i: (i, 0))
    out_shape = [jax.ShapeDtypeStruct((n_seq, seq_len, BRANCH_DIM), F32)] * 2
    out_specs = [row, row]
    if with_means:
        assert tm == MOBA_BLOCK
        out_shape.append(jax.ShapeDtypeStruct((n_seq, n_tiles, 1, BRANCH_DIM), F32))
        out_specs.append(pl.BlockSpec((None, None, 1, BRANCH_DIM), lambda b, i: (b, i, 0, 0)))
    return pl.pallas_call(
        functools.partial(_rope_kernel, tm=tm),
        grid=(n_seq, n_tiles),
        in_specs=[pl.BlockSpec((None, tm, BRANCH_DIM), lambda b, i: (b, i, 1)),
                  pl.BlockSpec((None, tm, BRANCH_DIM), lambda b, i: (b, i, 2)),
                  tab, tab, tab],
        out_specs=tuple(out_specs), out_shape=tuple(out_shape),
        compiler_params=_params(2),
        name="rope_qk",
    )(proj, proj, *tables)


def _moba_prompt_kernel(q_ref, k_ref, v_ref, km_ref, o_ref, *, n_blk, chunk):
    blk = MOBA_BLOCK
    bpc = chunk // blk
    qi = pl.program_id(2)
    qf = q_ref[...]
    st = _dot_nt_precise(km_ref[...], qf)
    n_row = lax.broadcasted_iota(I32, (n_blk, 1), 0)
    valid = n_row < qi
    sm = jnp.where(valid, st, -jnp.inf)
    rank = jnp.zeros(st.shape, I32)
    for m in range(n_blk):
        other = sm[m:m + 1, :]
        beats = (other > sm) | ((other == sm) & (m < n_row))
        rank = rank + jnp.where(beats & (m < qi), 1, 0)
    chosen = (valid & (rank < MOBA_TOPK)) | (n_row == qi)
    bias_t = jnp.where(chosen, 0.0, NEG_BIG).astype(BF16)

    qb = qf.astype(BF16)
    qpos = qi * blk + lax.broadcasted_iota(I32, (blk, 1), 0)
    krel = lax.broadcasted_iota(I32, (1, chunk), 1)

    def visit(c, carry, causal):
        m_i, l_i, acc = carry
        k0 = pl.multiple_of(c * chunk, chunk)
        kb = k_ref[pl.ds(k0, chunk), :].astype(BF16)
        vb = v_ref[pl.ds(k0, chunk), :].astype(BF16)
        widen = jnp.where(n_row == c * bpc + krel // blk, 1.0, 0.0).astype(BF16)
        bias = lax.dot_general(bias_t, widen, (((0,), (0,)), ((), ())), preferred_element_type=F32)
        lg = _dot_nt(qb, kb) * ATT_SCALE + bias
        if causal:
            lg = jnp.where(k0 + krel <= qpos, lg, NEG_BIG)
        m_new = jnp.maximum(m_i, jnp.max(lg, axis=-1, keepdims=True))
        alpha = jnp.exp(m_i - m_new)
        p = jnp.exp(lg - m_new)
        l_new = alpha * l_i + jnp.sum(p, axis=-1, keepdims=True)
        acc = alpha * acc + _dot(p.astype(BF16), vb)
        return m_new, l_new, acc

    init = (jnp.full((blk, 1), NEG_BIG, F32), jnp.zeros((blk, 1), F32), jnp.zeros((blk, HEAD_DIM), F32))
    own_chunk = qi // bpc
    carry = lax.fori_loop(0, own_chunk, lambda c, cr: visit(c, cr, False), init)
    _, l_f, acc = visit(own_chunk, carry, True)
    o_ref[...] = (acc / l_f).astype(o_ref.dtype)


def moba_prompt(q_rot, k_rot, proj, k_mean):
    n_seq, seq_len, _ = proj.shape
    n_blk = seq_len // MOBA_BLOCK
    chunk = min(seq_len, 4 * MOBA_BLOCK)
    assert seq_len % chunk == 0
    km = k_mean.reshape(n_seq, n_blk, BRANCH_DIM)
    tile = pl.BlockSpec((None, MOBA_BLOCK, HEAD_DIM), lambda b, h, i: (b, i, h))
    return pl.pallas_call(
        functools.partial(_moba_prompt_kernel, n_blk=n_blk, chunk=chunk),
        grid=(n_seq, N_HEADS, n_blk),
        in_specs=[tile,
                  pl.BlockSpec((None, seq_len, HEAD_DIM), lambda b, h, i: (b, 0, h)),
                  pl.BlockSpec((None, seq_len, HEAD_DIM), lambda b, h, i: (b, 0, 3 * N_HEADS + h)),
                  pl.BlockSpec((None, n_blk, HEAD_DIM), lambda b, h, i: (b, 0, h))],
        out_specs=tile,
        out_shape=jax.ShapeDtypeStruct((n_seq, seq_len, BRANCH_DIM), BF16),
        compiler_params=_params(3),
        name="moba_prompt",
    )(q_rot, k_rot, proj, km)


def _strict_lower(n):
    r = lax.broadcasted_iota(I32, (n, n), 0)
    c = lax.broadcasted_iota(I32, (n, n), 1)
    return jnp.where(r > c, 1.0, 0.0).astype(BF16)


def _sb_block(qb, kb, vb, tri, allowed, carry, acc):
    z = _dot_nt(qb, kb) * ATT_SCALE
    sp = _softplus_neg_abs(z)
    log_sig = jnp.minimum(z, 0.0) - sp
    log_1m = jnp.where(allowed, -jnp.maximum(z, 0.0) - sp, 0.0)
    later = _dot_exact_rhs(log_1m, tri) + carry
    a = jnp.where(allowed, jnp.exp(log_sig + later), 0.0)
    acc = acc + _dot(a.astype(BF16), vb)
    carry = carry + jnp.sum(log_1m, axis=-1, keepdims=True)
    return carry, acc


def _sb_prompt_kernel(q_ref, k_ref, v_ref, tri_ref, o_ref, *, tq, tk, back):
    qi = pl.program_id(1)
    win = tq + back
    qpos = qi * tq + lax.broadcasted_iota(I32, (tq, 1), 0)
    heads = [slice(h * HEAD_DIM, (h + 1) * HEAD_DIM) for h in range(N_HEADS)]
    qbs = [q_ref[:, cols].astype(BF16) for cols in heads]

    w0 = pl.multiple_of(jnp.maximum(qi * tq - back, 0), tk)
    allowed = (w0 + lax.broadcasted_iota(I32, (1, win), 1)) < qpos
    tri_win = tri_ref[...]
    carries, accs = [], []
    live = jnp.float32(SB_EXIT)
    for h, cols in enumerate(heads):
        kb = k_ref[pl.ds(w0, win), cols].astype(BF16)
        vb = v_ref[pl.ds(w0, win), cols].astype(BF16)
        carry, acc = _sb_block(qbs[h], kb, vb, tri_win, allowed,
                               jnp.zeros((tq, 1), F32), jnp.zeros((tq, HEAD_DIM), F32))
        carries.append(carry)
        accs.append(acc)
        live = jnp.maximum(live, jnp.max(carry))

    krel = lax.broadcasted_iota(I32, (1, tk), 1)
    tri_blk = tri_ref[0:tk, 0:tk]

    def cond(c):
        return (c[0] >= 0) & (c[1] > 0)

    def body(c):
        j, _, carries, accs = c
        k0 = pl.multiple_of(j * tk, tk)
        allowed = (k0 + krel) < qpos
        new_carries, new_accs = [], []
        live = jnp.float32(SB_EXIT)
        for h, cols in enumerate(heads):
            kb = k_ref[pl.ds(k0, tk), cols].astype(BF16)
            vb = v_ref[pl.ds(k0, tk), cols].astype(BF16)
            carry, acc = _sb_block(qbs[h], kb, vb, tri_blk, allowed, carries[h], accs[h])
            new_carries.append(carry)
            new_accs.append(acc)
            live = jnp.maximum(live, jnp.max(carry))
        return j - 1, (live > SB_EXIT).astype(I32), tuple(new_carries), tuple(new_accs)

    init = (w0 // tk - 1, (live > SB_EXIT).astype(I32), tuple(carries), tuple(accs))
    accs = lax.while_loop(cond, body, init)[3]
    for cols, acc in zip(heads, accs):
        o_ref[:, cols] = acc.astype(o_ref.dtype)


def sb_prompt(proj):
    n_seq, seq_len, _ = proj.shape
    tq, tk, back = 256, 128, 256
    win = tq + back
    assert seq_len >= win and seq_len % tq == 0
    idx = jnp.arange(win, dtype=I32)
    tri = (idx[:, None] > idx[None, :]).astype(BF16)
    seq = lambda blk: pl.BlockSpec((None, seq_len, BRANCH_DIM), lambda b, i: (b, 0, blk))
    return pl.pallas_call(
        functools.partial(_sb_prompt_kernel, tq=tq, tk=tk, back=back),
        grid=(n_seq, seq_len // tq),
        in_specs=[pl.BlockSpec((None, tq, BRANCH_DIM), lambda b, i: (b, i, 6)), seq(7), seq(8),
                  pl.BlockSpec((win, win), lambda b, i: (0, 0))],
        out_specs=pl.BlockSpec((None, tq, BRANCH_DIM), lambda b, i: (b, i, 0)),
        out_shape=jax.ShapeDtypeStruct((n_seq, seq_len, BRANCH_DIM), BF16),
        compiler_params=_params(2),
        name="sb_prompt",
    )(proj, proj, proj, tri)


PAGES_PER_STEP = 8
MEAN_PAGES_PER_STEP = 16
Q_PAD = 8
ROWS_PER_VREG = 8
PAGE_ROWS = PAGE_SIZE * N_HEADS
STACK = N_HEADS * Q_PAD


def _paged(cache):
    return cache.reshape(cache.shape[0], cache.shape[1], PAGE_ROWS // ROWS_PER_VREG, ROWS_PER_VREG, HEAD_DIM)


_PAGE_BLOCK = (None, None, PAGE_ROWS // ROWS_PER_VREG, ROWS_PER_VREG, HEAD_DIM)


def _page_rows(ref):
    return ref[...].reshape(PAGE_ROWS, HEAD_DIM)


def _page_mean_kernel(pt_ref, *refs):
    o_ref = refs[-1]
    for n in range(MEAN_PAGES_PER_STEP // 2):
        s = jnp.sum(refs[2 * n][...], axis=0) + jnp.sum(refs[2 * n + 1][...], axis=0)
        mean = (s[:N_HEADS] + s[N_HEADS:]) * (1.0 / MOBA_BLOCK)
        for h in range(N_HEADS):
            o_ref[h, n:n + 1, :] = mean[h:h + 1, :]


def page_block_means(cache, layer, page_table):
    n_seq, n_pages = page_table.shape
    pps = MEAN_PAGES_PER_STEP
    specs = [pl.BlockSpec(_PAGE_BLOCK, functools.partial(lambda b, s, pt, k: (layer, pt[b, s * pps + k], 0, 0, 0), k=k))
             for k in range(pps)]
    return pl.pallas_call(
        _page_mean_kernel,
        grid_spec=pltpu.PrefetchScalarGridSpec(
            num_scalar_prefetch=1,
            grid=(n_seq, n_pages // pps),
            in_specs=specs,
            out_specs=pl.BlockSpec((None, N_HEADS, pps // 2, HEAD_DIM), lambda b, s, pt: (b, 0, s, 0))),
        out_shape=jax.ShapeDtypeStruct((n_seq, N_HEADS, n_pages // 2, HEAD_DIM), F32),
        compiler_params=_params(2),
        name="page_block_means",
    )(page_table, *([_paged(cache)] * pps))


def _moba_select_kernel(q_ref, km_ref, o_ref, *, n_blk):
    lane = lax.broadcasted_iota(I32, (Q_PAD, LANES), 1)
    n_iota = lax.broadcasted_iota(I32, (Q_PAD, n_blk), 1)
    out = jnp.zeros((Q_PAD, LANES), I32)
    for h in range(N_HEADS):
        cols = slice(h * HEAD_DIM, (h + 1) * HEAD_DIM)
        s = _dot_nt_precise(q_ref[:, cols], km_ref[h])
        for slot in range(MOBA_TOPK):
            top = jnp.max(s, axis=-1, keepdims=True)
            idx = jnp.min(jnp.where(s == top, n_iota, n_blk), axis=-1, keepdims=True)
            s = jnp.where(n_iota == idx, -jnp.inf, s)
            out = jnp.where(lane == h * MOBA_TOPK + slot, idx, out)
    o_ref[...] = out


def moba_select(q_pad, k_mean):
    n_seq, _, n_blk, _ = k_mean.shape
    return pl.pallas_call(
        functools.partial(_moba_select_kernel, n_blk=n_blk),
        grid=(n_seq,),
        in_specs=[pl.BlockSpec((None, Q_PAD, BRANCH_DIM), lambda b: (b, 0, 0)),
                  pl.BlockSpec((None, N_HEADS, n_blk, HEAD_DIM), lambda b: (b, 0, 0, 0))],
        out_specs=pl.BlockSpec((None, Q_PAD, LANES), lambda b: (b, 0, 0)),
        out_shape=jax.ShapeDtypeStruct((n_seq, Q_PAD, LANES), I32),
        compiler_params=_params(1),
        name="moba_select",
    )(q_pad, k_mean)


def _moba_sample_kernel(ids_ref, pt_ref, q_ref, kn_ref, vn_ref, *refs, past_len):
    n_pg = 2 * MOBA_TOPK
    k_refs, v_refs, o_ref = refs[:n_pg], refs[n_pg:2 * n_pg], refs[2 * n_pg]
    head = pl.program_id(1)
    qi = pl.program_id(2)
    qb = q_ref[...].astype(BF16)
    qpos = past_len + lax.broadcasted_iota(I32, (Q_PAD, 1), 0)
    krel = lax.broadcasted_iota(I32, (1, PAGE_SIZE), 1)
    own_mask = (past_len + krel) <= qpos
    mine = (lax.broadcasted_iota(I32, (1, PAGE_ROWS), 1) % N_HEADS) == head
    logits = [jnp.where(mine, _dot_nt(qb, _page_rows(r).astype(BF16)) * ATT_SCALE, NEG_BIG) for r in k_refs]
    own = jnp.where(own_mask, _dot_nt(qb, kn_ref[...].astype(BF16)) * ATT_SCALE, NEG_BIG)
    top = jnp.max(own, axis=-1, keepdims=True)
    for lg in logits:
        top = jnp.maximum(top, jnp.max(lg, axis=-1, keepdims=True))
    p_own = jnp.where(own_mask, jnp.exp(own - top), 0.0)
    den = jnp.sum(p_own, axis=-1, keepdims=True)
    acc = _dot(p_own.astype(BF16), vn_ref[...].astype(BF16))
    for lg, vr in zip(logits, v_refs):
        p = jnp.where(mine, jnp.exp(lg - top), 0.0)
        den = den + jnp.sum(p, axis=-1, keepdims=True)
        acc = acc + _dot(p.astype(BF16), _page_rows(vr).astype(BF16))
    out = acc / den
    row = lax.broadcasted_iota(I32, (Q_PAD, 1), 0)
    o_ref[...] = jnp.sum(jnp.where(row == qi, out, 0.0), axis=0, keepdims=True).astype(o_ref.dtype)


def moba_sample(ids, page_table, q_pad, k_new, v_new, k_cache, v_cache, layer, *, n_q, past_len):
    n_seq = q_pad.shape[0]
    kc, vc = _paged(k_cache), _paged(v_cache)

    def page_map(b, h, q, ids_ref, pt_ref, *, slot, half):
        blk = ids_ref[((b * n_q + q) * N_HEADS + h) * MOBA_TOPK + slot]
        return (layer, pt_ref[b, 2 * blk + half], 0, 0, 0)

    page_specs = [pl.BlockSpec(_PAGE_BLOCK, functools.partial(page_map, slot=s, half=hf))
                  for s in range(MOBA_TOPK) for hf in range(2)]
    head = lambda rows: pl.BlockSpec((None, rows, HEAD_DIM), lambda b, h, q, i_, p_: (b, 0, h))
    out = pl.pallas_call(
        functools.partial(_moba_sample_kernel, past_len=past_len),
        grid_spec=pltpu.PrefetchScalarGridSpec(
            num_scalar_prefetch=2,
            grid=(n_seq, N_HEADS, n_q),
            in_specs=[head(Q_PAD), head(PAGE_SIZE), head(PAGE_SIZE)] + page_specs + page_specs,
            out_specs=pl.BlockSpec((None, None, 1, HEAD_DIM), lambda b, h, q, i_, p_: (b, q, 0, h))),
        out_shape=jax.ShapeDtypeStruct((n_seq, n_q, 1, BRANCH_DIM), BF16),
        compiler_params=_params(3),
        name="moba_sample",
    )(ids, page_table, q_pad, k_new, v_new, *([kc] * (2 * MOBA_TOPK)), *([vc] * (2 * MOBA_TOPK)))
    return out.reshape(n_seq * n_q, BRANCH_DIM)


def _sb_sample_kernel(pt_ref, q_ref, kn_ref, vn_ref, carry_in_ref, acc_in_ref, *refs, past_len, first_page, with_new):
    pps = PAGES_PER_STEP
    k_refs, v_refs = refs[:pps], refs[pps:2 * pps]
    acc_ref, carry_ref, live_ref = refs[2 * pps:]
    step = pl.program_id(1)
    new_steps = 1 if with_new else 0
    heads = [slice(h * HEAD_DIM, (h + 1) * HEAD_DIM) for h in range(N_HEADS)]

    def keep(carry, acc, rows):
        carry_ref[rows, :] = jnp.broadcast_to(carry, acc.shape)
        acc_ref[rows, :] = acc

    @pl.when(step == 0)
    def _():
        if with_new:
            qpos = past_len + lax.broadcasted_iota(I32, (Q_PAD, 1), 0)
            allowed = (past_len + lax.broadcasted_iota(I32, (1, PAGE_SIZE), 1)) < qpos
            tri = _strict_lower(PAGE_SIZE)
            live = jnp.float32(SB_EXIT)
            for h, cols in enumerate(heads):
                carry, acc = _sb_block(q_ref[:, cols].astype(BF16), kn_ref[:, cols].astype(BF16),
                                       vn_ref[:, cols].astype(BF16), tri, allowed,
                                       jnp.zeros((Q_PAD, 1), F32), jnp.zeros((Q_PAD, HEAD_DIM), F32))
                keep(carry, acc, slice(h * Q_PAD, (h + 1) * Q_PAD))
                live = jnp.maximum(live, jnp.max(carry))
        else:
            carry_ref[...] = carry_in_ref[...]
            acc_ref[...] = acc_in_ref[...]
            live = jnp.max(carry_in_ref[...])
        live_ref[0] = (live > SB_EXIT).astype(I32)

    def visit_page(k_ref, v_ref, k0):
        q_stack = jnp.concatenate([q_ref[:, cols] for cols in heads], axis=0).astype(BF16)
        row = lax.broadcasted_iota(I32, (STACK, 1), 0)
        col = lax.broadcasted_iota(I32, (1, PAGE_ROWS), 1)
        allowed = ((col % N_HEADS) == (row // Q_PAD)) & ((k0 + col // N_HEADS) < (past_len + row % Q_PAD))
        r = lax.broadcasted_iota(I32, (PAGE_ROWS, PAGE_ROWS), 0) // N_HEADS
        c = lax.broadcasted_iota(I32, (PAGE_ROWS, PAGE_ROWS), 1) // N_HEADS
        later_token = jnp.where(r > c, 1.0, 0.0).astype(BF16)
        carry, acc = _sb_block(q_stack, _page_rows(k_ref).astype(BF16), _page_rows(v_ref).astype(BF16),
                               later_token, allowed, carry_ref[:, 0:1], acc_ref[...])
        keep(carry, acc, slice(None))
        live_ref[0] = (jnp.max(carry) > SB_EXIT).astype(I32)

    for k in range(pps):
        @pl.when((step >= new_steps) & (live_ref[0] > 0))
        def _(k=k):
            page = first_page - ((step - new_steps) * pps + k)
            visit_page(k_refs[k], v_refs[k], page * PAGE_SIZE)


def _sb_sample_call(page_table, q_pad, k_new, v_new, carry, acc, kc, vc, layer, *, past_len, first_page,
                    n_page_steps, with_new):
    n_seq = page_table.shape[0]
    new_steps = 1 if with_new else 0

    def page_map(b, s, pt_ref, *, k):
        return (layer, pt_ref[b, first_page - (jnp.maximum(s - new_steps, 0) * PAGES_PER_STEP + k)], 0, 0, 0)

    page_specs = [pl.BlockSpec(_PAGE_BLOCK, functools.partial(page_map, k=k)) for k in range(PAGES_PER_STEP)]
    whole = lambda rows, width: pl.BlockSpec((None, rows, width), lambda b, s, pt: (b, 0, 0))
    state = whole(STACK, HEAD_DIM)
    state_shape = jax.ShapeDtypeStruct((n_seq, STACK, HEAD_DIM), F32)
    return pl.pallas_call(
        functools.partial(_sb_sample_kernel, past_len=past_len, first_page=first_page, with_new=with_new),
        grid_spec=pltpu.PrefetchScalarGridSpec(
            num_scalar_prefetch=1,
            grid=(n_seq, new_steps + n_page_steps),
            in_specs=[whole(Q_PAD, BRANCH_DIM), whole(PAGE_SIZE, BRANCH_DIM), whole(PAGE_SIZE, BRANCH_DIM),
                      state, state] + page_specs + page_specs,
            out_specs=(state, state),
            scratch_shapes=[pltpu.SMEM((1,), I32)]),
        out_shape=(state_shape, state_shape),
        compiler_params=_params(2),
        name="sb_sample",
    )(page_table, q_pad, k_new, v_new, carry, acc, *([kc] * PAGES_PER_STEP), *([vc] * PAGES_PER_STEP))


def sb_sample(page_table, q_pad, k_new, v_new, k_cache, v_cache, layer, *, past_len):
    n_seq, n_pages = page_table.shape
    kc, vc = _paged(k_cache), _paged(v_cache)
    zeros = jnp.zeros((n_seq, STACK, HEAD_DIM), F32)
    args = (page_table, q_pad, k_new, v_new)
    acc, carry = _sb_sample_call(*args, zeros, zeros, kc, vc, layer, past_len=past_len, first_page=n_pages - 1,
                                 n_page_steps=1, with_new=True)
    older = n_pages - PAGES_PER_STEP
    walk_older = lambda: _sb_sample_call(*args, carry, acc, kc, vc, layer, past_len=past_len, first_page=older - 1,
                                         n_page_steps=older // PAGES_PER_STEP, with_new=False)[0]
    acc = lax.cond(jnp.max(carry) > SB_EXIT, walk_older, lambda: acc)
    out = acc.reshape(n_seq, N_HEADS, Q_PAD, HEAD_DIM).transpose(0, 2, 1, 3)
    return out.reshape(n_seq, Q_PAD, BRANCH_DIM).astype(BF16)


def _gate_mix_kernel(u_ref, ha_ref, hb_ref, hc_ref, hd_ref, *refs):
    wg_refs, wp_refs = refs[:N_BRANCH], refs[N_BRANCH:2 * N_BRANCH]
    o_ref, wgb_ref, wpb_ref = refs[2 * N_BRANCH:]

    @pl.when(pl.program_id(1) == 0)
    def _():
        for b in range(N_BRANCH):
            wgb_ref[b] = wg_refs[b][...].astype(BF16)
            wpb_ref[b] = wp_refs[b][...].astype(BF16)

    u = u_ref[...]
    acc = None
    for b, h_ref in enumerate((ha_ref, hb_ref, hc_ref, hd_ref)):
        term = jax.nn.sigmoid(_dot(u, wgb_ref[b])) * _dot(h_ref[...], wpb_ref[b])
        acc = term if acc is None else acc + term
    o_ref[...] = acc.astype(o_ref.dtype)


def gate_mix(u, hs, w_in, projs, layer):
    m = u.shape[0]
    tm, tn = min(m, 512), 256
    gate_specs = [pl.BlockSpec((None, D_MODEL, tn),
                               functools.partial(lambda j, i, b: (layer, 0, (N_PROJ + b * D_MODEL) // tn + j), b=b))
                  for b in range(N_BRANCH)]
    proj_specs = [pl.BlockSpec((None, BRANCH_DIM, tn), lambda j, i: (layer, 0, j))] * N_BRANCH
    h_spec = pl.BlockSpec((tm, BRANCH_DIM), lambda j, i: (i, 0))
    return pl.pallas_call(
        _gate_mix_kernel,
        grid=(D_MODEL // tn, m // tm),
        in_specs=[pl.BlockSpec((tm, D_MODEL), lambda j, i: (i, 0))] + [h_spec] * N_BRANCH + gate_specs + proj_specs,
        out_specs=pl.BlockSpec((tm, tn), lambda j, i: (i, j)),
        out_shape=jax.ShapeDtypeStruct((m, D_MODEL), BF16),
        scratch_shapes=[pltpu.VMEM((N_BRANCH, D_MODEL, tn), BF16), pltpu.VMEM((N_BRANCH, BRANCH_DIM, tn), BF16)],
        compiler_params=_params(2),
        name="gate_mix",
    )(u, *hs, *([w_in] * N_BRANCH), *projs)


def _new_weights(te_ref, i):
    prev = te_ref[jnp.maximum(i - 1, 0)]
    return (i == 0) | (te_ref[i] != prev)


def _ffn_up_kernel(te_ref, nv_ref, xa_ref, xb_ref, w1_ref, w3_ref, o_ref, w1b_ref, w3b_ref):
    i = pl.program_id(1)

    @pl.when(_new_weights(te_ref, i))
    def _():
        w1b_ref[...] = w1_ref[...].astype(BF16)
        w3b_ref[...] = w3_ref[...].astype(BF16)

    @pl.when(i < nv_ref[0])
    def _():
        half = xa_ref.shape[1]
        xa, xb = xa_ref[...], xb_ref[...]
        a = _dot(xa, w1b_ref[:half]) + _dot(xb, w1b_ref[half:])
        g = _dot(xa, w3b_ref[:half]) + _dot(xb, w3b_ref[half:])
        o_ref[...] = (a * jax.nn.sigmoid(a) * g).astype(o_ref.dtype)

    @pl.when(i >= nv_ref[0])
    def _():
        o_ref[...] = jnp.zeros_like(o_ref)


def ffn_up(xa, xb, w1, w3, tile_expert, n_valid, tm, cols=(0, 0)):
    r = xa.shape[0]
    half = D_MODEL // 2
    tn = 512
    row = lambda c: (lambda j, i, te, nv: (jnp.minimum(i, nv[0] - 1), c))
    wmap = lambda j, i, te, nv: (te[i], 0, j)
    return pl.pallas_call(
        _ffn_up_kernel,
        grid_spec=pltpu.PrefetchScalarGridSpec(
            num_scalar_prefetch=2,
            grid=(D_FF // tn, r // tm),
            in_specs=[pl.BlockSpec((tm, half), row(cols[0])), pl.BlockSpec((tm, half), row(cols[1])),
                      pl.BlockSpec((None, D_MODEL, tn), wmap), pl.BlockSpec((None, D_MODEL, tn), wmap)],
            out_specs=pl.BlockSpec((tm, tn), lambda j, i, te, nv: (i, j)),
            scratch_shapes=[pltpu.VMEM((D_MODEL, tn), BF16), pltpu.VMEM((D_MODEL, tn), BF16)]),
        out_shape=jax.ShapeDtypeStruct((r, D_FF), BF16),
        compiler_params=_params(2),
        name="ffn_up",
    )(tile_expert, n_valid, xa, xb, w1, w3)


def _ffn_down_kernel(te_ref, nv_ref, h_ref, w_ref, *refs, gated):
    i = pl.program_id(1)
    if gated:
        g_ref, o_ref, wb_ref = refs
    else:
        o_ref, wb_ref = refs

    @pl.when(_new_weights(te_ref, i))
    def _():
        wb_ref[...] = w_ref[...].astype(BF16)

    @pl.when(i < nv_ref[0])
    def _():
        y = _dot(h_ref[...], wb_ref[...])
        o_ref[...] = y * g_ref[...] if gated else y

    @pl.when(i >= nv_ref[0])
    def _():
        o_ref[...] = jnp.zeros_like(o_ref)


def ffn_down(h, w2, tile_expert, n_valid, tm, gate=None):
    r = h.shape[0]
    tn = 512
    ins = [tile_expert, n_valid, h, w2]
    in_specs = [pl.BlockSpec((tm, D_FF), lambda j, i, te, nv: (jnp.minimum(i, nv[0] - 1), 0)),
                pl.BlockSpec((None, D_FF, tn), lambda j, i, te, nv: (te[i], 0, j))]
    if gate is not None:
        ins.append(gate)
        in_specs.append(pl.BlockSpec((tm, 1), lambda j, i, te, nv: (i, 0)))
    return pl.pallas_call(
        functools.partial(_ffn_down_kernel, gated=gate is not None),
        grid_spec=pltpu.PrefetchScalarGridSpec(
            num_scalar_prefetch=2,
            grid=(D_MODEL // tn, r // tm),
            in_specs=in_specs,
            out_specs=pl.BlockSpec((tm, tn), lambda j, i, te, nv: (i, j)),
            scratch_shapes=[pltpu.VMEM((D_FF, tn), BF16)]),
        out_shape=jax.ShapeDtypeStruct((r, D_MODEL), F32),
        compiler_params=_params(2),
        name="ffn_down",
    )(*ins)


def dense_ffn(u, w1, w3, w2, j):
    m = u.shape[0]
    tm = min(m, 512)
    te = jnp.full((m // tm,), j, I32)
    nv = jnp.full((1,), m // tm, I32)
    h = ffn_up(u, u, w1, w3, te, nv, tm, cols=(0, 1))
    return ffn_down(h, w2, te, nv, tm)


MOE_TM = 512


def _moe_gather_kernel(src_ref, x_ref, xa_ref, xb_ref, buf_ref, *, tm):
    base = pl.program_id(1) * tm

    def body(r, _):
        buf_ref[pl.ds(r, 1), :] = x_ref[pl.ds(src_ref[base + r], 1), :]
        return 0

    lax.fori_loop(0, tm, body, 0, unroll=8)
    w = buf_ref[...]
    xa_ref[...] = pltpu.bitcast(w & jnp.uint32(0xFFFF0000), F32).astype(BF16)
    xb_ref[...] = pltpu.bitcast(w << 16, F32).astype(BF16)


def moe_gather(packed, src_tok, n_rows):
    t, half = packed.shape
    tn = half // 2
    return pl.pallas_call(
        functools.partial(_moe_gather_kernel, tm=MOE_TM),
        grid_spec=pltpu.PrefetchScalarGridSpec(
            num_scalar_prefetch=1,
            grid=(half // tn, n_rows // MOE_TM),
            in_specs=[pl.BlockSpec((t, tn), lambda c, g, src: (0, c))],
            out_specs=(pl.BlockSpec((MOE_TM, tn), lambda c, g, src: (g, c)),
                       pl.BlockSpec((MOE_TM, tn), lambda c, g, src: (g, c))),
            scratch_shapes=[pltpu.VMEM((MOE_TM, tn), jnp.uint32)]),
        out_shape=(jax.ShapeDtypeStruct((n_rows, half), BF16), jax.ShapeDtypeStruct((n_rows, half), BF16)),
        compiler_params=_params(2),
        name="moe_gather",
    )(src_tok, packed)


def _moe_combine_kernel(p0_ref, p1_ref, y_ref, o_ref, *, tm):
    base = pl.program_id(1) * tm

    def body(r, _):
        o_ref[pl.ds(r, 1), :] = y_ref[pl.ds(p0_ref[base + r], 1), :] + y_ref[pl.ds(p1_ref[base + r], 1), :]
        return 0

    lax.fori_loop(0, tm, body, 0, unroll=8)


def moe_combine(y_sorted, pos0, pos1, n_tok, tm):
    r = y_sorted.shape[0]
    tn = 256
    return pl.pallas_call(
        functools.partial(_moe_combine_kernel, tm=tm),
        grid_spec=pltpu.PrefetchScalarGridSpec(
            num_scalar_prefetch=2,
            grid=(D_MODEL // tn, n_tok // tm),
            in_specs=[pl.BlockSpec((r, tn), lambda c, i, p0, p1: (0, c))],
            out_specs=pl.BlockSpec((tm, tn), lambda c, i, p0, p1: (i, c))),
        out_shape=jax.ShapeDtypeStruct((n_tok, D_MODEL), F32),
        compiler_params=_params(2),
        name="moe_combine",
    )(pos0, pos1, y_sorted)


def moe_ffn(packed, route, w1, w3, w2, j):
    t = packed.shape[0]
    n_pairs = 2 * t
    tm = MOE_TM
    n_tiles = -(-(n_pairs + N_EXPERTS * (tm - 1)) // tm)
    n_rows = n_tiles * tm
    expert = route[:, :2].astype(I32).reshape(n_pairs)
    gate = route[:, 2:4].reshape(n_pairs)
    onehot = (expert[:, None] == jnp.arange(N_EXPERTS, dtype=I32)[None, :]).astype(I32)
    running = jnp.cumsum(onehot, axis=0)
    counts = running[-1]
    rank = jnp.take_along_axis(running, expert[:, None], axis=1)[:, 0] - 1
    padded = ((counts + tm - 1) // tm) * tm
    ends = jnp.cumsum(padded)
    starts = ends - padded
    dest = starts[expert] + rank
    src_tok = jnp.zeros((n_rows,), I32).at[dest].set(jnp.arange(n_pairs, dtype=I32) // 2)
    gate_rows = jnp.zeros((n_rows,), F32).at[dest].set(gate)
    n_valid = (ends[-1] // tm).astype(I32).reshape(1)
    tile_start = jnp.arange(n_tiles, dtype=I32) * tm
    tile_expert = jnp.minimum(jnp.sum((ends[None, :] <= tile_start[:, None]).astype(I32), axis=1), N_EXPERTS - 1)
    last_expert = tile_expert[jnp.maximum(n_valid[0] - 1, 0)]
    tile_expert = jnp.where(tile_start < ends[-1], tile_expert, last_expert) + j * N_EXPERTS
    flat = lambda w: w.reshape((-1,) + w.shape[2:])

    xa, xb = moe_gather(packed, src_tok, n_rows)
    h = ffn_up(xa, xb, flat(w1), flat(w3), tile_expert, n_valid, tm)
    y = ffn_down(h, flat(w2), tile_expert, n_valid, tm, gate=gate_rows.reshape(n_rows, 1))
    t_pad = -(-t // tm) * tm
    pos = jnp.zeros((t_pad, 2), I32).at[:t].set(dest.reshape(t, 2))
    return moe_combine(y, pos[:, 0], pos[:, 1], t_pad, tm)


def kernel(x_prompt, x_sample, cache_moba_k, cache_moba_v, cache_sb_k, cache_sb_v, state_pool, state_conv,
           page_table, c_prompt, c_sample, w_ada, b_ada, w_in, pool_w, pool_scale, pool_proj, moba_proj,
           conv_w, conv_b, conv_ln_g, conv_ln_b, conv_pw, sb_proj, w_out, ln1_g, ln1_b, ln2_g, ln2_b,
           ffn_w1, ffn_w3, ffn_w2, moe_router, moe_router_b, moe_w1, moe_w3, moe_w2):
    bp, lp, _ = x_prompt.shape
    bs, ls, _ = x_sample.shape
    mp, ms = bp * lp, bs * ls
    past_len = page_table.shape[1] * PAGE_SIZE
    assert past_len // MOBA_BLOCK >= MOBA_TOPK and ls <= Q_PAD and lp % MOBA_BLOCK == 0

    c_all = jnp.zeros((16, D_MODEL), F32).at[:bp].set(c_prompt).at[bp:bp + bs].set(c_sample)
    mod = ada_mod(c_all, w_ada, b_ada).reshape(DEPTH, 16, 6, D_MODEL)

    def mods(l, k):
        mp_ = mod[l, :bp, k].reshape(bp, 1, D_MODEL)
        ms_ = jnp.repeat(mod[l, bp:bp + bs, k], ls, axis=0).reshape(1, ms, D_MODEL)
        return mp_, ms_

    pos_p = jnp.arange(lp, dtype=I32)
    pos_s = past_len + jnp.arange(ls, dtype=I32)
    tab_p, tab_s = rope_tables(pos_p), rope_tables(pos_s)

    xp = x_prompt.reshape(mp, D_MODEL)
    xs = x_sample.reshape(ms, D_MODEL)
    sc1, sh1 = mods(0, 1), mods(0, 0)
    up = ln_mod(xp, sc1[0], sh1[0], lp)
    us = ln_mod(xs, sc1[1], sh1[1], ls)
    zero_pool = jnp.zeros((bp, POOL_HALO, BRANCH_DIM), F32)
    zero_conv = jnp.zeros((bp, CONV_HALO, BRANCH_DIM), F32)
    outs_p = [[] for _ in range(6)]
    outs_s = [[] for _ in range(6)]

    def head4(t, b, l):
        return t.reshape(b, l, N_HEADS, HEAD_DIM)

    for l in range(DEPTH):
        g1p, g1s = mods(l, 2)
        sc2, sh2 = mods(l, 4), mods(l, 3)
        g2p, g2s = mods(l, 5)
        projs = (pool_proj, moba_proj, conv_pw, sb_proj)

        p3 = matmul(up, w_in, l, n=N_PROJ).reshape(bp, lp, N_PROJ)
        ha = pool_branch(p3, zero_pool, pool_w[l], pool_scale[l], pos0=0)
        hc, glu = conv_branch(p3, zero_conv, conv_w[l], conv_b[l], conv_ln_g[l], conv_ln_b[l])
        q_rot, k_rot, k_mean = rope_qk(p3, tab_p, with_means=True)
        hb = moba_prompt(q_rot, k_rot, p3, k_mean)
        hd = sb_prompt(p3)
        flat_p = lambda t: t.reshape(mp, BRANCH_DIM)
        mix = gate_mix(up, (flat_p(ha), flat_p(hb), flat_p(hc), flat_p(hd)), w_in, projs, l)
        attn = matmul(mix, w_out, l)
        outs_p[0].append(head4(k_rot, bp, lp))
        outs_p[1].append(head4(p3[:, :, 3 * BRANCH_DIM:4 * BRANCH_DIM], bp, lp))
        outs_p[2].append(head4(p3[:, :, 7 * BRANCH_DIM:8 * BRANCH_DIM], bp, lp))
        outs_p[3].append(head4(p3[:, :, 8 * BRANCH_DIM:9 * BRANCH_DIM], bp, lp))
        outs_p[4].append(p3[:, lp - POOL_BUF:, :BRANCH_DIM])
        outs_p[5].append(glu[:, lp - CONV_BUF:])

        ps3 = matmul(us, w_in, l, n=N_PROJ).reshape(bs, ls, N_PROJ)
        pool_hist = jnp.concatenate([state_pool[l], ps3[:, :, :BRANCH_DIM]], axis=1)
        pool_init = jnp.pad(state_pool[l], ((0, 0), (POOL_HALO - POOL_BUF, 0), (0, 0)))
        has = pool_branch(ps3, pool_init, pool_w[l], pool_scale[l], pos0=past_len)
        conv_init = jnp.pad(state_conv[l], ((0, 0), (CONV_HALO - CONV_BUF, 0), (0, 0)))
        hcs, glu_s = conv_branch(ps3, conv_init, conv_w[l], conv_b[l], conv_ln_g[l], conv_ln_b[l])
        conv_hist = jnp.concatenate([state_conv[l], glu_s], axis=1)
        qs_rot, ks_rot = rope_qk(ps3, tab_s, with_means=False)
        pad_q = lambda t: jnp.pad(t, ((0, 0), (0, Q_PAD - ls), (0, 0)))
        pad_kv = lambda t: jnp.pad(t, ((0, 0), (0, PAGE_SIZE - ls), (0, 0)))
        mv_s = ps3[:, :, 3 * BRANCH_DIM:4 * BRANCH_DIM]
        sq_s, sk_s, sv_s = (ps3[:, :, k * BRANCH_DIM:(k + 1) * BRANCH_DIM] for k in (6, 7, 8))
        k_mean_s = page_block_means(cache_moba_k, l, page_table)
        picks = moba_select(pad_q(qs_rot), k_mean_s)
        ids = picks[:, :ls, :N_HEADS * MOBA_TOPK].reshape(-1)
        hbs = moba_sample(ids, page_table, pad_q(qs_rot), pad_kv(ks_rot), pad_kv(mv_s),
                          cache_moba_k, cache_moba_v, l, n_q=ls, past_len=past_len)
        hds = sb_sample(page_table, pad_q(sq_s), pad_kv(sk_s), pad_kv(sv_s), cache_sb_k, cache_sb_v, l,
                        past_len=past_len)[:, :ls].reshape(ms, BRANCH_DIM)
        flat_s = lambda t: t.reshape(ms, BRANCH_DIM)
        mix_s = gate_mix(us, (flat_s(has), hbs, flat_s(hcs), hds), w_in, projs, l)
        attn_s = matmul(mix_s, w_out, l)
        outs_s[0].append(head4(ks_rot, bs, ls))
        outs_s[1].append(head4(mv_s, bs, ls))
        outs_s[2].append(head4(sk_s, bs, ls))
        outs_s[3].append(head4(sv_s, bs, ls))
        outs_s[4].append(pool_hist[:, -POOL_BUF:])
        outs_s[5].append(conv_hist[:, -CONV_BUF:])

        j = l // 2
        if l % 2 == 0:
            xp, u2p = resid_ln(xp, attn, g1p, ln1_g[l], ln1_b[l], lp, sc2[0], sh2[0])
            xs, u2s = resid_ln(xs, attn_s, g1s, ln1_g[l], ln1_b[l], ls, sc2[1], sh2[1])
            fp = dense_ffn(u2p, ffn_w1, ffn_w3, ffn_w2, j)
            fs = dense_ffn(u2s, ffn_w1, ffn_w3, ffn_w2, j)
        else:
            router = (moe_router[j], moe_router_b[j])
            xp, pk_p, rt_p = resid_ln(xp, attn, g1p, ln1_g[l], ln1_b[l], lp, sc2[0], sh2[0], router)
            xs, pk_s, rt_s = resid_ln(xs, attn_s, g1s, ln1_g[l], ln1_b[l], ls, sc2[1], sh2[1], router)
            f_all = moe_ffn(jnp.concatenate([pk_p, pk_s], axis=0), jnp.concatenate([rt_p, rt_s], axis=0),
                            moe_w1, moe_w3, moe_w2, j)
            fp, fs = f_all, f_all
        fs_row0 = 0 if l % 2 == 0 else mp
        if l + 1 < DEPTH:
            nsc, nsh = mods(l + 1, 1), mods(l + 1, 0)
            xp, up = resid_ln(xp, fp, g2p, ln2_g[l], ln2_b[l], lp, nsc[0], nsh[0])
            xs, us = resid_ln(xs, fs, g2s, ln2_g[l], ln2_b[l], ls, nsc[1], nsh[1], y_row0=fs_row0)
        else:
            xp = resid_ln(xp, fp, g2p, ln2_g[l], ln2_b[l], lp)
            xs = resid_ln(xs, fs, g2s, ln2_g[l], ln2_b[l], ls, y_row0=fs_row0)

    stack = lambda group: tuple(jnp.stack(t) for t in group)
    return (xp.reshape(bp, lp, D_MODEL), xs.reshape(bs, ls, D_MODEL)) + stack(outs_p) + stack(outs_s)
```

```python
import functools
import math

import jax
import jax.numpy as jnp
from jax import lax
from jax.experimental import pallas as pl
from jax.experimental.pallas import tpu as pltpu

F32 = jnp.float32
BF16 = jnp.bfloat16
I32 = jnp.int32

D_MODEL = 2048
DEPTH = 4
PAGE_SIZE = 128
HEAD_DIM = 128
BRANCH_DIM = 512
N_HEADS = BRANCH_DIM // HEAD_DIM
POOL_WINDOWS = (2, 4, 8, 16)
POOL_BUF = 15
POOL_HALO = 16
MOBA_BLOCK = 256
MOBA_TOPK = 3
ROPE_THETA = 500000.0
ROPE_DIM = HEAD_DIM // 4
CONV_WIDTH = 31
CONV_BUF = CONV_WIDTH - 1
CONV_HALO = 32
N_BRANCH = 4
D_FF = 5632
N_EXPERTS = 8
ALPHA = (2.0 * DEPTH) ** 0.25
LN_EPS = 1e-5
N_PROJ = 9 * BRANCH_DIM
ATT_SCALE = HEAD_DIM ** -0.5

SB_EXIT = -104.0
NEG_BIG = -1e30

VMEM_LIMIT_V7X = 56 * 1024 * 1024
LANES = 128


def _params(n_axes):
    return pltpu.CompilerParams(dimension_semantics=("arbitrary",) * n_axes,
                                vmem_limit_bytes=VMEM_LIMIT_V7X)


def _ln(x):
    mu = jnp.mean(x, axis=-1, keepdims=True)
    xc = x - mu
    var = jnp.mean(xc * xc, axis=-1, keepdims=True)
    return xc * lax.rsqrt(var + LN_EPS)


def _dot(a, b):
    return jnp.dot(a, b, preferred_element_type=F32)


def _dot_nt(a, b):
    return lax.dot_general(a, b, (((1,), (1,)), ((), ())), preferred_element_type=F32)


def _split2(x):
    hi = x.astype(BF16)
    lo = (x - hi.astype(F32)).astype(BF16)
    return hi, lo


def _dot_nt_precise(a, b):
    ah, al = _split2(a)
    bh, bl = _split2(b)
    return _dot_nt(ah, bh) + (_dot_nt(ah, bl) + _dot_nt(al, bh))


def _dot_exact_rhs(a, b_bf16):
    hi, lo = _split2(a)
    return _dot(hi, b_bf16) + _dot(lo, b_bf16)


def _softplus_neg_abs(z):
    return jnp.log1p(jnp.exp(-jnp.abs(z)))


def _ada_kernel(c_ref, w_ref, b_ref, o_ref):
    c = c_ref[...]
    s = (c * jax.nn.sigmoid(c)).astype(BF16)
    o_ref[...] = _dot(s, w_ref[...].astype(BF16)) + b_ref[...]


def ada_mod(c_all, w_ada, b_ada):
    rows = c_all.shape[0]
    tn = 1024
    n = w_ada.shape[-1]
    return pl.pallas_call(
        _ada_kernel,
        grid=(DEPTH, n // tn),
        in_specs=[pl.BlockSpec((rows, D_MODEL), lambda l, j: (0, 0)),
                  pl.BlockSpec((None, D_MODEL, tn), lambda l, j: (l, 0, j)),
                  pl.BlockSpec((None, 1, tn), lambda l, j: (l, 0, j))],
        out_specs=pl.BlockSpec((None, rows, tn), lambda l, j: (l, 0, j)),
        out_shape=jax.ShapeDtypeStruct((DEPTH, rows, n), F32),
        compiler_params=_params(2),
        name="ada_mod",
    )(c_all, w_ada, b_ada.reshape(DEPTH, 1, n))


def _mod_spec(mod, tm, tiles_per_group):
    if mod.shape[1] == 1:
        return pl.BlockSpec((None, 1, D_MODEL), lambda i: (i // tiles_per_group, 0, 0))
    return pl.BlockSpec((None, tm, D_MODEL), lambda i: (0, i, 0))


def _ln_mod_kernel(x_ref, sc_ref, sh_ref, o_ref):
    o_ref[...] = (_ln(x_ref[...]) * (1.0 + sc_ref[...]) + sh_ref[...]).astype(o_ref.dtype)


def ln_mod(x, sc, sh, rows_per_group):
    m = x.shape[0]
    tm = min(m, 256)
    tpg = max(rows_per_group // tm, 1)
    row = pl.BlockSpec((tm, D_MODEL), lambda i: (i, 0))
    return pl.pallas_call(
        _ln_mod_kernel,
        grid=(m // tm,),
        in_specs=[row, _mod_spec(sc, tm, tpg), _mod_spec(sh, tm, tpg)],
        out_specs=row,
        out_shape=jax.ShapeDtypeStruct((m, D_MODEL), BF16),
        compiler_params=_params(1),
        name="ln_mod",
    )(x, sc, sh)


def _pack_bf16_pairs(u):
    half = u.shape[1] // 2
    a = pltpu.bitcast(u[:, :half].astype(BF16).astype(F32), jnp.uint32)
    b = pltpu.bitcast(u[:, half:].astype(BF16).astype(F32), jnp.uint32)
    return (a & jnp.uint32(0xFFFF0000)) | (b >> 16)


def _route_top2(u, rh_ref, rl_ref, rb_ref):
    uh, ul = _split2(u)
    lg = _dot(uh, rh_ref[...]) + (_dot(uh, rl_ref[...]) + _dot(ul, rh_ref[...])) + rb_ref[...]
    lane = lax.broadcasted_iota(I32, lg.shape, 1)
    m1 = jnp.max(lg, axis=-1, keepdims=True)
    i1 = jnp.min(jnp.where(lg == m1, lane, LANES), axis=-1, keepdims=True)
    lg2 = jnp.where(lane == i1, -jnp.inf, lg)
    m2 = jnp.max(lg2, axis=-1, keepdims=True)
    i2 = jnp.min(jnp.where(lg2 == m2, lane, LANES), axis=-1, keepdims=True)
    e2 = jnp.exp(m2 - m1)
    g1 = 1.0 / (1.0 + e2)
    g2 = e2 / (1.0 + e2)
    out = jnp.where(lane == 0, i1.astype(F32), 0.0)
    out = jnp.where(lane == 1, i2.astype(F32), out)
    out = jnp.where(lane == 2, g1, out)
    out = jnp.where(lane == 3, g2, out)
    return out


def _resid_ln_kernel(*refs, mode):
    x_ref, y_ref, g_ref, lg_ref, lb_ref = refs[:5]
    t = ALPHA * x_ref[...] + g_ref[...] * y_ref[...]
    xn = _ln(t) * lg_ref[...] + lb_ref[...]
    if mode == "last":
        refs[5][...] = xn
        return
    sc_ref, sh_ref = refs[5:7]
    u = _ln(xn) * (1.0 + sc_ref[...]) + sh_ref[...]
    if mode == "next":
        xo_ref, uo_ref = refs[7:9]
        xo_ref[...] = xn
        uo_ref[...] = u.astype(BF16)
    else:
        rh_ref, rl_ref, rb_ref, xo_ref, up_ref, rt_ref = refs[7:13]
        xo_ref[...] = xn
        up_ref[...] = _pack_bf16_pairs(u)
        rt_ref[...] = _route_top2(u, rh_ref, rl_ref, rb_ref)


def resid_ln(x, y, g, ln_g, ln_b, rows_per_group, sc=None, sh=None, router=None, y_row0=0):
    m = x.shape[0]
    tm = min(m, 256)
    tpg = max(rows_per_group // tm, 1)
    assert y_row0 % tm == 0
    row = pl.BlockSpec((tm, D_MODEL), lambda i: (i, 0))
    vec = pl.BlockSpec((1, D_MODEL), lambda i: (0, 0))
    ins = [x, y, g, ln_g.reshape(1, D_MODEL), ln_b.reshape(1, D_MODEL)]
    in_specs = [row, pl.BlockSpec((tm, D_MODEL), lambda i: (y_row0 // tm + i, 0)), _mod_spec(g, tm, tpg), vec, vec]
    x_out = jax.ShapeDtypeStruct((m, D_MODEL), F32)
    if sc is None:
        mode, out_shape, out_specs = "last", x_out, row
    else:
        ins += [sc, sh]
        in_specs += [_mod_spec(sc, tm, tpg), _mod_spec(sh, tm, tpg)]
        if router is None:
            mode = "next"
            out_shape = (x_out, jax.ShapeDtypeStruct((m, D_MODEL), BF16))
            out_specs = (row, row)
        else:
            mode = "route"
            rw, rb = router
            rw_pad = jnp.zeros((D_MODEL, LANES), F32).at[:, :N_EXPERTS].set(rw)
            rhi = rw_pad.astype(BF16)
            rlo = (rw_pad - rhi.astype(F32)).astype(BF16)
            rb_pad = jnp.full((1, LANES), -jnp.inf, F32).at[0, :N_EXPERTS].set(rb)
            ins += [rhi, rlo, rb_pad]
            full = pl.BlockSpec((D_MODEL, LANES), lambda i: (0, 0))
            in_specs += [full, full, pl.BlockSpec((1, LANES), lambda i: (0, 0))]
            out_shape = (x_out, jax.ShapeDtypeStruct((m, D_MODEL // 2), jnp.uint32),
                         jax.ShapeDtypeStruct((m, LANES), F32))
            out_specs = (row, pl.BlockSpec((tm, D_MODEL // 2), lambda i: (i, 0)),
                         pl.BlockSpec((tm, LANES), lambda i: (i, 0)))
    return pl.pallas_call(
        functools.partial(_resid_ln_kernel, mode=mode),
        grid=(m // tm,),
        in_specs=in_specs, out_specs=out_specs, out_shape=out_shape,
        compiler_params=_params(1),
        name="resid_ln_" + mode,
    )(*ins)


def _mm_kernel(x_ref, w_ref, o_ref, wb_ref):
    @pl.when(pl.program_id(1) == 0)
    def _():
        wb_ref[...] = w_ref[...].astype(BF16)
    o_ref[...] = _dot(x_ref[...], wb_ref[...]).astype(o_ref.dtype)


def matmul(x, w, layer, *, col0=0, n=None, tn=512, tm=1024, out_dtype=F32):
    m, k = x.shape
    n = w.shape[-1] if n is None else n
    tm = min(m, tm)
    cb = col0 // tn
    return pl.pallas_call(
        _mm_kernel,
        grid=(n // tn, m // tm),
        in_specs=[pl.BlockSpec((tm, k), lambda j, i: (i, 0)),
                  pl.BlockSpec((None, k, tn), lambda j, i: (layer, 0, cb + j))],
        out_specs=pl.BlockSpec((tm, tn), lambda j, i: (i, j)),
        out_shape=jax.ShapeDtypeStruct((m, n), out_dtype),
        scratch_shapes=[pltpu.VMEM((k, tn), BF16)],
        compiler_params=_params(2),
        name="matmul",
    )(x, w)


def _pool_kernel(x_ref, init_ref, wg_ref, scale_ref, o_ref, halo_ref, ext_ref, *, tm, n_tiles, pos0):
    i = pl.program_id(1)

    @pl.when(i == 0)
    def _():
        halo_ref[...] = init_ref[...]

    x = x_ref[...]
    ext_ref[0:POOL_HALO, :] = halo_ref[...]
    ext_ref[POOL_HALO:POOL_HALO + tm, :] = x
    if n_tiles > 1:
        halo_ref[...] = x[tm - POOL_HALO:, :]
    pos = pos0 + i * tm + lax.broadcasted_iota(I32, (tm, 1), 0)
    for g, w in enumerate(POOL_WINDOWS):
        cols = pl.ds(g * LANES, LANES)
        s = ext_ref[pl.ds(POOL_HALO, tm), cols]
        for j in range(1, w):
            s = s + ext_ref[pl.ds(POOL_HALO - j, tm), cols]
        cnt = jnp.minimum(pos + 1, w).astype(F32)
        d = s / cnt - x[:, g * LANES:(g + 1) * LANES]
        y = _dot(d.astype(BF16), wg_ref[g].astype(BF16))
        o_ref[:, g * LANES:(g + 1) * LANES] = (y * scale_ref[:, g * LANES:(g + 1) * LANES]).astype(o_ref.dtype)


def pool_branch(proj, init, w_grp, scale, *, pos0):
    n_seq, seq_len, _ = proj.shape
    tm = min(seq_len, 256)
    n_tiles = seq_len // tm
    return pl.pallas_call(
        functools.partial(_pool_kernel, tm=tm, n_tiles=n_tiles, pos0=pos0),
        grid=(n_seq, n_tiles),
        in_specs=[pl.BlockSpec((None, tm, BRANCH_DIM), lambda b, i: (b, i, 0)),
                  pl.BlockSpec((None, POOL_HALO, BRANCH_DIM), lambda b, i: (b, 0, 0)),
                  pl.BlockSpec((len(POOL_WINDOWS), LANES, LANES), lambda b, i: (0, 0, 0)),
                  pl.BlockSpec((1, BRANCH_DIM), lambda b, i: (0, 0))],
        out_specs=pl.BlockSpec((None, tm, BRANCH_DIM), lambda b, i: (b, i, 0)),
        out_shape=jax.ShapeDtypeStruct((n_seq, seq_len, BRANCH_DIM), BF16),
        scratch_shapes=[pltpu.VMEM((POOL_HALO, BRANCH_DIM), F32),
                        pltpu.VMEM((POOL_HALO + tm, BRANCH_DIM), F32)],
        compiler_params=_params(2),
        name="pool_branch",
    )(proj, init, w_grp, scale.reshape(1, BRANCH_DIM))


def _conv_kernel(a_ref, init_ref, w_ref, b_ref, g_ref, beta_ref, o_ref, h_ref,
                 halo_ref, ext_ref, y_ref, *, tm, n_tiles, rc):
    i = pl.program_id(1)

    @pl.when(i == 0)
    def _():
        halo_ref[...] = init_ref[...]

    a = a_ref[...]
    h = a[:, :BRANCH_DIM] * jax.nn.sigmoid(a[:, BRANCH_DIM:])
    h_ref[...] = h
    ext_ref[0:CONV_HALO, :] = halo_ref[...]
    ext_ref[CONV_HALO:CONV_HALO + tm, :] = h
    if n_tiles > 1:
        halo_ref[...] = h[tm - CONV_HALO:, :]
    off = CONV_HALO - CONV_BUF
    for c in range(BRANCH_DIM // LANES):
        cols = pl.ds(c * LANES, LANES)
        for r0 in range(0, tm, rc):
            acc = jnp.zeros((rc, LANES), F32) + b_ref[:, c * LANES:(c + 1) * LANES]
            for k in range(CONV_WIDTH):
                acc = acc + ext_ref[pl.ds(r0 + off + k, rc), cols] * w_ref[k:k + 1, c * LANES:(c + 1) * LANES]
            y_ref[pl.ds(r0, rc), cols] = acc
    y = _ln(y_ref[...]) * g_ref[...] + beta_ref[...]
    o_ref[...] = (y * jax.nn.sigmoid(y)).astype(o_ref.dtype)


def conv_branch(proj, init, w, b, g, beta):
    n_seq, seq_len, _ = proj.shape
    tm = min(seq_len, 128)
    rc = min(tm, 64)
    n_tiles = seq_len // tm
    col_block = (4 * BRANCH_DIM) // (2 * BRANCH_DIM)
    w_pad = jnp.zeros((CONV_HALO, BRANCH_DIM), F32).at[:CONV_WIDTH].set(w)
    vec = pl.BlockSpec((1, BRANCH_DIM), lambda bb, i: (0, 0))
    row = pl.BlockSpec((None, tm, BRANCH_DIM), lambda bb, i: (bb, i, 0))
    return pl.pallas_call(
        functools.partial(_conv_kernel, tm=tm, n_tiles=n_tiles, rc=rc),
        grid=(n_seq, n_tiles),
        in_specs=[pl.BlockSpec((None, tm, 2 * BRANCH_DIM), lambda bb, i: (bb, i, col_block)),
                  pl.BlockSpec((None, CONV_HALO, BRANCH_DIM), lambda bb, i: (bb, 0, 0)),
                  pl.BlockSpec((CONV_HALO, BRANCH_DIM), lambda bb, i: (0, 0)),
                  vec, vec, vec],
        out_specs=(row, row),
        out_shape=(jax.ShapeDtypeStruct((n_seq, seq_len, BRANCH_DIM), BF16),
                   jax.ShapeDtypeStruct((n_seq, seq_len, BRANCH_DIM), F32)),
        scratch_shapes=[pltpu.VMEM((CONV_HALO, BRANCH_DIM), F32),
                        pltpu.VMEM((CONV_HALO + tm, BRANCH_DIM), F32),
                        pltpu.VMEM((tm, BRANCH_DIM), F32)],
        compiler_params=_params(2),
        name="conv_branch",
    )(proj, init, w_pad, b.reshape(1, -1), g.reshape(1, -1), beta.reshape(1, -1))


def rope_tables(pos):
    half = ROPE_DIM // 2
    inv_freq = ROPE_THETA ** (-jnp.arange(half, dtype=F32) / half)
    ang = pos.astype(F32)[:, None] * inv_freq[None, :]
    cos, sin = jnp.cos(ang), jnp.sin(ang)
    n = pos.shape[0]
    z = jnp.zeros((n, half), F32)
    rest = HEAD_DIM - ROPE_DIM
    cos_t = jnp.concatenate([cos, cos, jnp.ones((n, rest), F32)], axis=1)
    sin_up = jnp.concatenate([-sin, z, jnp.zeros((n, rest), F32)], axis=1)
    sin_dn = jnp.concatenate([z, sin, jnp.zeros((n, rest), F32)], axis=1)
    return cos_t, sin_up, sin_dn


def _rope(x, cos_t, sin_up, sin_dn):
    half = ROPE_DIM // 2
    width = x.shape[1]
    tile = lambda t: jnp.concatenate([t] * (width // HEAD_DIM), axis=1)
    up = pltpu.roll(x, width - half, axis=1)
    dn = pltpu.roll(x, half, axis=1)
    return x * tile(cos_t) + up * tile(sin_up) + dn * tile(sin_dn)


def _rope_kernel(q_ref, k_ref, cos_ref, su_ref, sd_ref, qo_ref, ko_ref, *maybe_km_ref, tm):
    cos_t, su, sd = cos_ref[...], su_ref[...], sd_ref[...]
    qo_ref[...] = _rope(q_ref[...], cos_t, su, sd)
    kr = _rope(k_ref[...], cos_t, su, sd)
    ko_ref[...] = kr
    if maybe_km_ref:
        maybe_km_ref[0][...] = jnp.sum(kr, axis=0, keepdims=True) * (1.0 / tm)


def rope_qk(proj, tables, *, with_means):
    n_seq, seq_len, _ = proj.shape
    tm = min(seq_len, MOBA_BLOCK)
    n_tiles = seq_len // tm
    row = pl.BlockSpec((None, tm, BRANCH_DIM), lambda b, i: (b, i, 0))
    tab = pl.BlockSpec((tm, HEAD_DIM), lambda b, i: (i, 0))
    out_shape = [jax.ShapeDtypeStruct((n_seq, seq_len, BRANCH_DIM), F32)] * 2
    out_specs = [row, row]
    if with_means:
        assert tm == MOBA_BLOCK
        out_shape.append(jax.ShapeDtypeStruct((n_seq, n_tiles, 1, BRANCH_DIM), F32))
        out_specs.append(pl.BlockSpec((None, None, 1, BRANCH_DIM), lambda b, i: (b, i, 0, 0)))
    return pl.pallas_call(
        functools.partial(_rope_kernel, tm=tm),
        grid=(n_seq, n_tiles),
        in_specs=[pl.BlockSpec((None, tm, BRANCH_DIM), lambda b, i: (b, i, 1)),
                  pl.BlockSpec((None, tm, BRANCH_DIM), lambda b, i: (b, i, 2)),
                  tab, tab, tab],
        out_specs=tuple(out_specs), out_shape=tuple(out_shape),
        compiler_params=_params(2),
        name="rope_qk",
    )(proj, proj, *tables)


def _moba_prompt_kernel(q_ref, k_ref, v_ref, km_ref, o_ref, *, n_blk, chunk):
    blk = MOBA_BLOCK
    bpc = chunk // blk
    qi = pl.program_id(2)
    qf = q_ref[...]
    st = _dot_nt_precise(km_ref[...], qf)
    n_row = lax.broadcasted_iota(I32, (n_blk, 1), 0)
    valid = n_row < qi
    sm = jnp.where(valid, st, -jnp.inf)
    rank = jnp.zeros(st.shape, I32)
    for m in range(n_blk):
        other = sm[m:m + 1, :]
        beats = (other > sm) | ((other == sm) & (m < n_row))
        rank = rank + jnp.where(beats & (m < qi), 1, 0)
    chosen = (valid & (rank < MOBA_TOPK)) | (n_row == qi)
    bias_t = jnp.where(chosen, 0.0, NEG_BIG).astype(BF16)

    qb = qf.astype(BF16)
    qpos = qi * blk + lax.broadcasted_iota(I32, (blk, 1), 0)
    krel = lax.broadcasted_iota(I32, (1, chunk), 1)

    def visit(c, carry, causal):
        m_i, l_i, acc = carry
        k0 = pl.multiple_of(c * chunk, chunk)
        kb = k_ref[pl.ds(k0, chunk), :].astype(BF16)
        vb = v_ref[pl.ds(k0, chunk), :].astype(BF16)
        widen = jnp.where(n_row == c * bpc + krel // blk, 1.0, 0.0).astype(BF16)
        bias = lax.dot_general(bias_t, widen, (((0,), (0,)), ((), ())), preferred_element_type=F32)
        lg = _dot_nt(qb, kb) * ATT_SCALE + bias
        if causal:
            lg = jnp.where(k0 + krel <= qpos, lg, NEG_BIG)
        m_new = jnp.maximum(m_i, jnp.max(lg, axis=-1, keepdims=True))
        alpha = jnp.exp(m_i - m_new)
        p = jnp.exp(lg - m_new)
        l_new = alpha * l_i + jnp.sum(p, axis=-1, keepdims=True)
        acc = alpha * acc + _dot(p.astype(BF16), vb)
        return m_new, l_new, acc

    init = (jnp.full((blk, 1), NEG_BIG, F32), jnp.zeros((blk, 1), F32), jnp.zeros((blk, HEAD_DIM), F32))
    own_chunk = qi // bpc
    carry = lax.fori_loop(0, own_chunk, lambda c, cr: visit(c, cr, False), init)
    _, l_f, acc = visit(own_chunk, carry, True)
    o_ref[...] = (acc / l_f).astype(o_ref.dtype)


def moba_prompt(q_rot, k_rot, proj, k_mean):
    n_seq, seq_len, _ = proj.shape
    n_blk = seq_len // MOBA_BLOCK
    chunk = min(seq_len, 4 * MOBA_BLOCK)
    assert seq_len % chunk == 0
    km = k_mean.reshape(n_seq, n_blk, BRANCH_DIM)
    tile = pl.BlockSpec((None, MOBA_BLOCK, HEAD_DIM), lambda b, h, i: (b, i, h))
    return pl.pallas_call(
        functools.partial(_moba_prompt_kernel, n_blk=n_blk, chunk=chunk),
        grid=(n_seq, N_HEADS, n_blk),
        in_specs=[tile,
                  pl.BlockSpec((None, seq_len, HEAD_DIM), lambda b, h, i: (b, 0, h)),
                  pl.BlockSpec((None, seq_len, HEAD_DIM), lambda b, h, i: (b, 0, 3 * N_HEADS + h)),
                  pl.BlockSpec((None, n_blk, HEAD_DIM), lambda b, h, i: (b, 0, h))],
        out_specs=tile,
        out_shape=jax.ShapeDtypeStruct((n_seq, seq_len, BRANCH_DIM), BF16),
        compiler_params=_params(3),
        name="moba_prompt",
    )(q_rot, k_rot, proj, km)


def _strict_lower(n):
    r = lax.broadcasted_iota(I32, (n, n), 0)
    c = lax.broadcasted_iota(I32, (n, n), 1)
    return jnp.where(r > c, 1.0, 0.0).astype(BF16)


def _sb_block(qb, kb, vb, tri, allowed, carry, acc):
    z = _dot_nt(qb, kb) * ATT_SCALE
    sp = _softplus_neg_abs(z)
    log_sig = jnp.minimum(z, 0.0) - sp
    log_1m = jnp.where(allowed, -jnp.maximum(z, 0.0) - sp, 0.0)
    later = _dot_exact_rhs(log_1m, tri) + carry
    a = jnp.where(allowed, jnp.exp(log_sig + later), 0.0)
    acc = acc + _dot(a.astype(BF16), vb)
    carry = carry + jnp.sum(log_1m, axis=-1, keepdims=True)
    return carry, acc


def _sb_prompt_kernel(q_ref, k_ref, v_ref, tri_ref, o_ref, *, tq, tk, back):
    qi = pl.program_id(1)
    win = tq + back
    qpos = qi * tq + lax.broadcasted_iota(I32, (tq, 1), 0)
    heads = [slice(h * HEAD_DIM, (h + 1) * HEAD_DIM) for h in range(N_HEADS)]
    qbs = [q_ref[:, cols].astype(BF16) for cols in heads]

    w0 = pl.multiple_of(jnp.maximum(qi * tq - back, 0), tk)
    allowed = (w0 + lax.broadcasted_iota(I32, (1, win), 1)) < qpos
    tri_win = tri_ref[...]
    carries, accs = [], []
    live = jnp.float32(SB_EXIT)
    for h, cols in enumerate(heads):
        kb = k_ref[pl.ds(w0, win), cols].astype(BF16)
        vb = v_ref[pl.ds(w0, win), cols].astype(BF16)
        carry, acc = _sb_block(qbs[h], kb, vb, tri_win, allowed,
                               jnp.zeros((tq, 1), F32), jnp.zeros((tq, HEAD_DIM), F32))
        carries.append(carry)
        accs.append(acc)
        live = jnp.maximum(live, jnp.max(carry))

    krel = lax.broadcasted_iota(I32, (1, tk), 1)
    tri_blk = tri_ref[0:tk, 0:tk]

    def cond(c):
        return (c[0] >= 0) & (c[1] > 0)

    def body(c):
        j, _, carries, accs = c
        k0 = pl.multiple_of(j * tk, tk)
        allowed = (k0 + krel) < qpos
        new_carries, new_accs = [], []
        live = jnp.float32(SB_EXIT)
        for h, cols in enumerate(heads):
            kb = k_ref[pl.ds(k0, tk), cols].astype(BF16)
            vb = v_ref[pl.ds(k0, tk), cols].astype(BF16)
            carry, acc = _sb_block(qbs[h], kb, vb, tri_blk, allowed, carries[h], accs[h])
            new_carries.append(carry)
            new_accs.append(acc)
            live = jnp.maximum(live, jnp.max(carry))
        return j - 1, (live > SB_EXIT).astype(I32), tuple(new_carries), tuple(new_accs)

    init = (w0 // tk - 1, (live > SB_EXIT).astype(I32), tuple(carries), tuple(accs))
    accs = lax.while_loop(cond, body, init)[3]
    for cols, acc in zip(heads, accs):
        o_ref[:, cols] = acc.astype(o_ref.dtype)


def sb_prompt(proj):
    n_seq, seq_len, _ = proj.shape
    tq, tk, back = 256, 128, 256
    win = tq + back
    assert seq_len >= win and seq_len % tq == 0
    idx = jnp.arange(win, dtype=I32)
    tri = (idx[:, None] > idx[None, :]).astype(BF16)
    seq = lambda blk: pl.BlockSpec((None, seq_len, BRANCH_DIM), lambda b, i: (b, 0, blk))
    return pl.pallas_call(
        functools.partial(_sb_prompt_kernel, tq=tq, tk=tk, back=back),
        grid=(n_seq, seq_len // tq),
        in_specs=[pl.BlockSpec((None, tq, BRANCH_DIM), lambda b, i: (b, i, 6)), seq(7), seq(8),
                  pl.BlockSpec((win, win), lambda b, i: (0, 0))],
        out_specs=pl.BlockSpec((None, tq, BRANCH_DIM), lambda b, i: (b, i, 0)),
        out_shape=jax.ShapeDtypeStruct((n_seq, seq_len, BRANCH_DIM), BF16),
        compiler_params=_params(2),
        name="sb_prompt",
    )(proj, proj, proj, tri)


PAGES_PER_STEP = 8
MEAN_PAGES_PER_STEP = 16
Q_PAD = 8
ROWS_PER_VREG = 8
PAGE_ROWS = PAGE_SIZE * N_HEADS
STACK = N_HEADS * Q_PAD


def _paged(cache):
    return cache.reshape(cache.shape[0], cache.shape[1], PAGE_ROWS // ROWS_PER_VREG, ROWS_PER_VREG, HEAD_DIM)


_PAGE_BLOCK = (None, None, PAGE_ROWS // ROWS_PER_VREG, ROWS_PER_VREG, HEAD_DIM)


def _page_rows(ref):
    return ref[...].reshape(PAGE_ROWS, HEAD_DIM)


def _page_mean_kernel(pt_ref, *refs):
    o_ref = refs[-1]
    for n in range(MEAN_PAGES_PER_STEP // 2):
        s = jnp.sum(refs[2 * n][...], axis=0) + jnp.sum(refs[2 * n + 1][...], axis=0)
        mean = (s[:N_HEADS] + s[N_HEADS:]) * (1.0 / MOBA_BLOCK)
        for h in range(N_HEADS):
            o_ref[h, n:n + 1, :] = mean[h:h + 1, :]


def page_block_means(cache, layer, page_table):
    n_seq, n_pages = page_table.shape
    pps = MEAN_PAGES_PER_STEP
    specs = [pl.BlockSpec(_PAGE_BLOCK, functools.partial(lambda b, s, pt, k: (layer, pt[b, s * pps + k], 0, 0, 0), k=k))
             for k in range(pps)]
    return pl.pallas_call(
        _page_mean_kernel,
        grid_spec=pltpu.PrefetchScalarGridSpec(
            num_scalar_prefetch=1,
            grid=(n_seq, n_pages // pps),
            in_specs=specs,
            out_specs=pl.BlockSpec((None, N_HEADS, pps // 2, HEAD_DIM), lambda b, s, pt: (b, 0, s, 0))),
        out_shape=jax.ShapeDtypeStruct((n_seq, N_HEADS, n_pages // 2, HEAD_DIM), F32),
        compiler_params=_params(2),
        name="page_block_means",
    )(page_table, *([_paged(cache)] * pps))


def _moba_select_kernel(q_ref, km_ref, o_ref, *, n_blk):
    lane = lax.broadcasted_iota(I32, (Q_PAD, LANES), 1)
    n_iota = lax.broadcasted_iota(I32, (Q_PAD, n_blk), 1)
    out = jnp.zeros((Q_PAD, LANES), I32)
    for h in range(N_HEADS):
        cols = slice(h * HEAD_DIM, (h + 1) * HEAD_DIM)
        s = _dot_nt_precise(q_ref[:, cols], km_ref[h])
        for slot in range(MOBA_TOPK):
            top = jnp.max(s, axis=-1, keepdims=True)
            idx = jnp.min(jnp.where(s == top, n_iota, n_blk), axis=-1, keepdims=True)
            s = jnp.where(n_iota == idx, -jnp.inf, s)
            out = jnp.where(lane == h * MOBA_TOPK + slot, idx, out)
    o_ref[...] = out


def moba_select(q_pad, k_mean):
    n_seq, _, n_blk, _ = k_mean.shape
    return pl.pallas_call(
        functools.partial(_moba_select_kernel, n_blk=n_blk),
        grid=(n_seq,),
        in_specs=[pl.BlockSpec((None, Q_PAD, BRANCH_DIM), lambda b: (b, 0, 0)),
                  pl.BlockSpec((None, N_HEADS, n_blk, HEAD_DIM), lambda b: (b, 0, 0, 0))],
        out_specs=pl.BlockSpec((None, Q_PAD, LANES), lambda b: (b, 0, 0)),
        out_shape=jax.ShapeDtypeStruct((n_seq, Q_PAD, LANES), I32),
        compiler_params=_params(1),
        name="moba_select",
    )(q_pad, k_mean)


def _moba_sample_kernel(ids_ref, pt_ref, q_ref, kn_ref, vn_ref, *refs, past_len):
    n_pg = 2 * MOBA_TOPK
    k_refs, v_refs, o_ref = refs[:n_pg], refs[n_pg:2 * n_pg], refs[2 * n_pg]
    head = pl.program_id(1)
    qi = pl.program_id(2)
    qb = q_ref[...].astype(BF16)
    qpos = past_len + lax.broadcasted_iota(I32, (Q_PAD, 1), 0)
    krel = lax.broadcasted_iota(I32, (1, PAGE_SIZE), 1)
    own_mask = (past_len + krel) <= qpos
    mine = (lax.broadcasted_iota(I32, (1, PAGE_ROWS), 1) % N_HEADS) == head
    logits = [jnp.where(mine, _dot_nt(qb, _page_rows(r).astype(BF16)) * ATT_SCALE, NEG_BIG) for r in k_refs]
    own = jnp.where(own_mask, _dot_nt(qb, kn_ref[...].astype(BF16)) * ATT_SCALE, NEG_BIG)
    top = jnp.max(own, axis=-1, keepdims=True)
    for lg in logits:
        top = jnp.maximum(top, jnp.max(lg, axis=-1, keepdims=True))
    p_own = jnp.where(own_mask, jnp.exp(own - top), 0.0)
    den = jnp.sum(p_own, axis=-1, keepdims=True)
    acc = _dot(p_own.astype(BF16), vn_ref[...].astype(BF16))
    for lg, vr in zip(logits, v_refs):
        p = jnp.where(mine, jnp.exp(lg - top), 0.0)
        den = den + jnp.sum(p, axis=-1, keepdims=True)
        acc = acc + _dot(p.astype(BF16), _page_rows(vr).astype(BF16))
    out = acc / den
    row = lax.broadcasted_iota(I32, (Q_PAD, 1), 0)
    o_ref[...] = jnp.sum(jnp.where(row == qi, out, 0.0), axis=0, keepdims=True).astype(o_ref.dtype)


def moba_sample(ids, page_table, q_pad, k_new, v_new, k_cache, v_cache, layer, *, n_q, past_len):
    n_seq = q_pad.shape[0]
    kc, vc = _paged(k_cache), _paged(v_cache)

    def page_map(b, h, q, ids_ref, pt_ref, *, slot, half):
        blk = ids_ref[((b * n_q + q) * N_HEADS + h) * MOBA_TOPK + slot]
        return (layer, pt_ref[b, 2 * blk + half], 0, 0, 0)

    page_specs = [pl.BlockSpec(_PAGE_BLOCK, functools.partial(page_map, slot=s, half=hf))
                  for s in range(MOBA_TOPK) for hf in range(2)]
    head = lambda rows: pl.BlockSpec((None, rows, HEAD_DIM), lambda b, h, q, i_, p_: (b, 0, h))
    out = pl.pallas_call(
        functools.partial(_moba_sample_kernel, past_len=past_len),
        grid_spec=pltpu.PrefetchScalarGridSpec(
            num_scalar_prefetch=2,
            grid=(n_seq, N_HEADS, n_q),
            in_specs=[head(Q_PAD), head(PAGE_SIZE), head(PAGE_SIZE)] + page_specs + page_specs,
            out_specs=pl.BlockSpec((None, None, 1, HEAD_DIM), lambda b, h, q, i_, p_: (b, q, 0, h))),
        out_shape=jax.ShapeDtypeStruct((n_seq, n_q, 1, BRANCH_DIM), BF16),
        compiler_params=_params(3),
        name="moba_sample",
    )(ids, page_table, q_pad, k_new, v_new, *([kc] * (2 * MOBA_TOPK)), *([vc] * (2 * MOBA_TOPK)))
    return out.reshape(n_seq * n_q, BRANCH_DIM)


def _sb_sample_kernel(pt_ref, q_ref, kn_ref, vn_ref, carry_in_ref, acc_in_ref, *refs, past_len, first_page, with_new):
    pps = PAGES_PER_STEP
    k_refs, v_refs = refs[:pps], refs[pps:2 * pps]
    acc_ref, carry_ref, live_ref = refs[2 * pps:]
    step = pl.program_id(1)
    new_steps = 1 if with_new else 0
    heads = [slice(h * HEAD_DIM, (h + 1) * HEAD_DIM) for h in range(N_HEADS)]

    def keep(carry, acc, rows):
        carry_ref[rows, :] = jnp.broadcast_to(carry, acc.shape)
        acc_ref[rows, :] = acc

    @pl.when(step == 0)
    def _():
        if with_new:
            qpos = past_len + lax.broadcasted_iota(I32, (Q_PAD, 1), 0)
            allowed = (past_len + lax.broadcasted_iota(I32, (1, PAGE_SIZE), 1)) < qpos
            tri = _strict_lower(PAGE_SIZE)
            live = jnp.float32(SB_EXIT)
            for h, cols in enumerate(heads):
                carry, acc = _sb_block(q_ref[:, cols].astype(BF16), kn_ref[:, cols].astype(BF16),
                                       vn_ref[:, cols].astype(BF16), tri, allowed,
                                       jnp.zeros((Q_PAD, 1), F32), jnp.zeros((Q_PAD, HEAD_DIM), F32))
                keep(carry, acc, slice(h * Q_PAD, (h + 1) * Q_PAD))
                live = jnp.maximum(live, jnp.max(carry))
        else:
            carry_ref[...] = carry_in_ref[...]
            acc_ref[...] = acc_in_ref[...]
            live = jnp.max(carry_in_ref[...])
        live_ref[0] = (live > SB_EXIT).astype(I32)

    def visit_page(k_ref, v_ref, k0):
        q_stack = jnp.concatenate([q_ref[:, cols] for cols in heads], axis=0).astype(BF16)
        row = lax.broadcasted_iota(I32, (STACK, 1), 0)
        col = lax.broadcasted_iota(I32, (1, PAGE_ROWS), 1)
        allowed = ((col % N_HEADS) == (row // Q_PAD)) & ((k0 + col // N_HEADS) < (past_len + row % Q_PAD))
        r = lax.broadcasted_iota(I32, (PAGE_ROWS, PAGE_ROWS), 0) // N_HEADS
        c = lax.broadcasted_iota(I32, (PAGE_ROWS, PAGE_ROWS), 1) // N_HEADS
        later_token = jnp.where(r > c, 1.0, 0.0).astype(BF16)
        carry, acc = _sb_block(q_stack, _page_rows(k_ref).astype(BF16), _page_rows(v_ref).astype(BF16),
                               later_token, allowed, carry_ref[:, 0:1], acc_ref[...])
        keep(carry, acc, slice(None))
        live_ref[0] = (jnp.max(carry) > SB_EXIT).astype(I32)

    for k in range(pps):
        @pl.when((step >= new_steps) & (live_ref[0] > 0))
        def _(k=k):
            page = first_page - ((step - new_steps) * pps + k)
            visit_page(k_refs[k], v_refs[k], page * PAGE_SIZE)


def _sb_sample_call(page_table, q_pad, k_new, v_new, carry, acc, kc, vc, layer, *, past_len, first_page,
                    n_page_steps, with_new):
    n_seq = page_table.shape[0]
    new_steps = 1 if with_new else 0

    def page_map(b, s, pt_ref, *, k):
        return (layer, pt_ref[b, first_page - (jnp.maximum(s - new_steps, 0) * PAGES_PER_STEP + k)], 0, 0, 0)

    page_specs = [pl.BlockSpec(_PAGE_BLOCK, functools.partial(page_map, k=k)) for k in range(PAGES_PER_STEP)]
    whole = lambda rows, width: pl.BlockSpec((None, rows, width), lambda b, s, pt: (b, 0, 0))
    state = whole(STACK, HEAD_DIM)
    state_shape = jax.ShapeDtypeStruct((n_seq, STACK, HEAD_DIM), F32)
    return pl.pallas_call(
        functools.partial(_sb_sample_kernel, past_len=past_len, first_page=first_page, with_new=with_new),
        grid_spec=pltpu.PrefetchScalarGridSpec(
            num_scalar_prefetch=1,
            grid=(n_seq, new_steps + n_page_steps),
            in_specs=[whole(Q_PAD, BRANCH_DIM), whole(PAGE_SIZE, BRANCH_DIM), whole(PAGE_SIZE, BRANCH_DIM),
                      state, state] + page_specs + page_specs,
            out_specs=(state, state),
            scratch_shapes=[pltpu.SMEM((1,), I32)]),
        out_shape=(state_shape, state_shape),
        compiler_params=_params(2),
        name="sb_sample",
    )(page_table, q_pad, k_new, v_new, carry, acc, *([kc] * PAGES_PER_STEP), *([vc] * PAGES_PER_STEP))


def sb_sample(page_table, q_pad, k_new, v_new, k_cache, v_cache, layer, *, past_len):
    n_seq, n_pages = page_table.shape
    kc, vc = _paged(k_cache), _paged(v_cache)
    zeros = jnp.zeros((n_seq, STACK, HEAD_DIM), F32)
    args = (page_table, q_pad, k_new, v_new)
    acc, carry = _sb_sample_call(*args, zeros, zeros, kc, vc, layer, past_len=past_len, first_page=n_pages - 1,
                                 n_page_steps=1, with_new=True)
    older = n_pages - PAGES_PER_STEP
    walk_older = lambda: _sb_sample_call(*args, carry, acc, kc, vc, layer, past_len=past_len, first_page=older - 1,
                                         n_page_steps=older // PAGES_PER_STEP, with_new=False)[0]
    acc = lax.cond(jnp.max(carry) > SB_EXIT, walk_older, lambda: acc)
    out = acc.reshape(n_seq, N_HEADS, Q_PAD, HEAD_DIM).transpose(0, 2, 1, 3)
    return out.reshape(n_seq, Q_PAD, BRANCH_DIM).astype(BF16)


def _gate_mix_kernel(u_ref, ha_ref, hb_ref, hc_ref, hd_ref, *refs):
    wg_refs, wp_refs = refs[:N_BRANCH], refs[N_BRANCH:2 * N_BRANCH]
    o_ref, wgb_ref, wpb_ref = refs[2 * N_BRANCH:]

    @pl.when(pl.program_id(1) == 0)
    def _():
        for b in range(N_BRANCH):
            wgb_ref[b] = wg_refs[b][...].astype(BF16)
            wpb_ref[b] = wp_refs[b][...].astype(BF16)

    u = u_ref[...]
    acc = None
    for b, h_ref in enumerate((ha_ref, hb_ref, hc_ref, hd_ref)):
        term = jax.nn.sigmoid(_dot(u, wgb_ref[b])) * _dot(h_ref[...], wpb_ref[b])
        acc = term if acc is None else acc + term
    o_ref[...] = acc.astype(o_ref.dtype)


def gate_mix(u, hs, w_in, projs, layer):
    m = u.shape[0]
    tm, tn = min(m, 1024), 256
    gate_specs = [pl.BlockSpec((None, D_MODEL, tn),
                               functools.partial(lambda j, i, b: (layer, 0, (N_PROJ + b * D_MODEL) // tn + j), b=b))
                  for b in range(N_BRANCH)]
    proj_specs = [pl.BlockSpec((None, BRANCH_DIM, tn), lambda j, i: (layer, 0, j))] * N_BRANCH
    h_spec = pl.BlockSpec((tm, BRANCH_DIM), lambda j, i: (i, 0))
    return pl.pallas_call(
        _gate_mix_kernel,
        grid=(D_MODEL // tn, m // tm),
        in_specs=[pl.BlockSpec((tm, D_MODEL), lambda j, i: (i, 0))] + [h_spec] * N_BRANCH + gate_specs + proj_specs,
        out_specs=pl.BlockSpec((tm, tn), lambda j, i: (i, j)),
        out_shape=jax.ShapeDtypeStruct((m, D_MODEL), BF16),
        scratch_shapes=[pltpu.VMEM((N_BRANCH, D_MODEL, tn), BF16), pltpu.VMEM((N_BRANCH, BRANCH_DIM, tn), BF16)],
        compiler_params=_params(2),
        name="gate_mix",
    )(u, *hs, *([w_in] * N_BRANCH), *projs)


def _new_weights(te_ref, i):
    prev = te_ref[jnp.maximum(i - 1, 0)]
    return (i == 0) | (te_ref[i] != prev)


def _ffn_up_kernel(te_ref, nv_ref, xa_ref, xb_ref, w1_ref, w3_ref, o_ref, w1b_ref, w3b_ref):
    i = pl.program_id(1)

    @pl.when(_new_weights(te_ref, i))
    def _():
        w1b_ref[...] = w1_ref[...].astype(BF16)
        w3b_ref[...] = w3_ref[...].astype(BF16)

    @pl.when(i < nv_ref[0])
    def _():
        half = xa_ref.shape[1]
        xa, xb = xa_ref[...], xb_ref[...]
        a = _dot(xa, w1b_ref[:half]) + _dot(xb, w1b_ref[half:])
        g = _dot(xa, w3b_ref[:half]) + _dot(xb, w3b_ref[half:])
        o_ref[...] = (a * jax.nn.sigmoid(a) * g).astype(o_ref.dtype)

    @pl.when(i >= nv_ref[0])
    def _():
        o_ref[...] = jnp.zeros_like(o_ref)


def ffn_up(xa, xb, w1, w3, tile_expert, n_valid, tm, cols=(0, 0)):
    r = xa.shape[0]
    half = D_MODEL // 2
    tn = 512
    row = lambda c: (lambda j, i, te, nv: (jnp.minimum(i, nv[0] - 1), c))
    wmap = lambda j, i, te, nv: (te[i], 0, j)
    return pl.pallas_call(
        _ffn_up_kernel,
        grid_spec=pltpu.PrefetchScalarGridSpec(
            num_scalar_prefetch=2,
            grid=(D_FF // tn, r // tm),
            in_specs=[pl.BlockSpec((tm, half), row(cols[0])), pl.BlockSpec((tm, half), row(cols[1])),
                      pl.BlockSpec((None, D_MODEL, tn), wmap), pl.BlockSpec((None, D_MODEL, tn), wmap)],
            out_specs=pl.BlockSpec((tm, tn), lambda j, i, te, nv: (i, j)),
            scratch_shapes=[pltpu.VMEM((D_MODEL, tn), BF16), pltpu.VMEM((D_MODEL, tn), BF16)]),
        out_shape=jax.ShapeDtypeStruct((r, D_FF), BF16),
        compiler_params=_params(2),
        name="ffn_up",
    )(tile_expert, n_valid, xa, xb, w1, w3)


def _ffn_down_kernel(te_ref, nv_ref, h_ref, w_ref, o_ref, wb_ref):
    i = pl.program_id(1)

    @pl.when(_new_weights(te_ref, i))
    def _():
        wb_ref[...] = w_ref[...].astype(BF16)

    @pl.when(i < nv_ref[0])
    def _():
        o_ref[...] = _dot(h_ref[...], wb_ref[...])

    @pl.when(i >= nv_ref[0])
    def _():
        o_ref[...] = jnp.zeros_like(o_ref)


def ffn_down(h, w2, tile_expert, n_valid, tm):
    r = h.shape[0]
    tn = 512
    return pl.pallas_call(
        _ffn_down_kernel,
        grid_spec=pltpu.PrefetchScalarGridSpec(
            num_scalar_prefetch=2,
            grid=(D_MODEL // tn, r // tm),
            in_specs=[pl.BlockSpec((tm, D_FF), lambda j, i, te, nv: (jnp.minimum(i, nv[0] - 1), 0)),
                      pl.BlockSpec((None, D_FF, tn), lambda j, i, te, nv: (te[i], 0, j))],
            out_specs=pl.BlockSpec((tm, tn), lambda j, i, te, nv: (i, j)),
            scratch_shapes=[pltpu.VMEM((D_FF, tn), BF16)]),
        out_shape=jax.ShapeDtypeStruct((r, D_MODEL), F32),
        compiler_params=_params(2),
        name="ffn_down",
    )(tile_expert, n_valid, h, w2)


def dense_ffn(u, w1, w3, w2, j):
    m = u.shape[0]
    tm = min(m, 512)
    te = jnp.full((m // tm,), j, I32)
    nv = jnp.full((1,), m // tm, I32)
    h = ffn_up(u, u, w1, w3, te, nv, tm, cols=(0, 1))
    return ffn_down(h, w2, te, nv, tm)


MOE_TM = 512


def _moe_gather_kernel(src_ref, x_ref, xa_ref, xb_ref, buf_ref, *, tm):
    base = pl.program_id(1) * tm

    def body(r, _):
        buf_ref[pl.ds(r, 1), :] = x_ref[pl.ds(src_ref[base + r], 1), :]
        return 0

    lax.fori_loop(0, tm, body, 0, unroll=8)
    w = buf_ref[...]
    xa_ref[...] = pltpu.bitcast(w & jnp.uint32(0xFFFF0000), F32).astype(BF16)
    xb_ref[...] = pltpu.bitcast(w << 16, F32).astype(BF16)


def moe_gather(packed, src_tok, n_rows):
    t, half = packed.shape
    tn = half // 2
    return pl.pallas_call(
        functools.partial(_moe_gather_kernel, tm=MOE_TM),
        grid_spec=pltpu.PrefetchScalarGridSpec(
            num_scalar_prefetch=1,
            grid=(half // tn, n_rows // MOE_TM),
            in_specs=[pl.BlockSpec((t, tn), lambda c, g, src: (0, c))],
            out_specs=(pl.BlockSpec((MOE_TM, tn), lambda c, g, src: (g, c)),
                       pl.BlockSpec((MOE_TM, tn), lambda c, g, src: (g, c))),
            scratch_shapes=[pltpu.VMEM((MOE_TM, tn), jnp.uint32)]),
        out_shape=(jax.ShapeDtypeStruct((n_rows, half), BF16), jax.ShapeDtypeStruct((n_rows, half), BF16)),
        compiler_params=_params(2),
        name="moe_gather",
    )(src_tok, packed)


COMBINE_TM = 256


def _moe_combine_kernel(p0_ref, p1_ref, rt_ref, y_hbm, o_ref, a_ref, b_ref, sem, *, tm):
    base = pl.program_id(0) * tm

    def row_copy(pos_ref, buf_ref, slot, r):
        return pltpu.make_async_copy(y_hbm.at[pl.ds(pos_ref[base + r], 1)], buf_ref.at[pl.ds(r, 1)], sem.at[slot])

    def start(r, _):
        row_copy(p0_ref, a_ref, 0, r).start()
        row_copy(p1_ref, b_ref, 1, r).start()
        return 0

    def wait(r, _):
        row_copy(p0_ref, a_ref, 0, r).wait()
        row_copy(p1_ref, b_ref, 1, r).wait()
        return 0

    lax.fori_loop(0, tm, start, 0, unroll=8)
    lax.fori_loop(0, tm, wait, 0, unroll=8)
    o_ref[...] = rt_ref[:, 2:3] * a_ref[...] + rt_ref[:, 3:4] * b_ref[...]


def moe_combine(y_sorted, pos0, pos1, route):
    n_tok = route.shape[0]
    tm = COMBINE_TM
    return pl.pallas_call(
        functools.partial(_moe_combine_kernel, tm=tm),
        grid_spec=pltpu.PrefetchScalarGridSpec(
            num_scalar_prefetch=2,
            grid=(n_tok // tm,),
            in_specs=[pl.BlockSpec((tm, LANES), lambda i, p0, p1: (i, 0)),
                      pl.BlockSpec(memory_space=pl.ANY)],
            out_specs=pl.BlockSpec((tm, D_MODEL), lambda i, p0, p1: (i, 0)),
            scratch_shapes=[pltpu.VMEM((tm, D_MODEL), F32), pltpu.VMEM((tm, D_MODEL), F32),
                            pltpu.SemaphoreType.DMA((2,))]),
        out_shape=jax.ShapeDtypeStruct((n_tok, D_MODEL), F32),
        compiler_params=_params(1),
        name="moe_combine",
    )(pos0, pos1, route, y_sorted)


def moe_ffn(packed, route, w1, w3, w2, j):
    t = packed.shape[0]
    n_pairs = 2 * t
    tm = MOE_TM
    n_tiles = -(-(n_pairs + N_EXPERTS * (tm - 1)) // tm)
    n_rows = n_tiles * tm
    expert = route[:, :2].astype(I32).reshape(n_pairs)
    onehot = (expert[:, None] == jnp.arange(N_EXPERTS, dtype=I32)[None, :]).astype(I32)
    running = jnp.cumsum(onehot, axis=0)
    counts = running[-1]
    rank = jnp.take_along_axis(running, expert[:, None], axis=1)[:, 0] - 1
    padded = ((counts + tm - 1) // tm) * tm
    ends = jnp.cumsum(padded)
    starts = ends - padded
    dest = starts[expert] + rank
    src_tok = jnp.zeros((n_rows,), I32).at[dest].set(jnp.arange(n_pairs, dtype=I32) // 2)
    n_valid = (ends[-1] // tm).astype(I32).reshape(1)
    tile_start = jnp.arange(n_tiles, dtype=I32) * tm
    tile_expert = jnp.minimum(jnp.sum((ends[None, :] <= tile_start[:, None]).astype(I32), axis=1), N_EXPERTS - 1)
    last_expert = tile_expert[jnp.maximum(n_valid[0] - 1, 0)]
    tile_expert = jnp.where(tile_start < ends[-1], tile_expert, last_expert) + j * N_EXPERTS
    flat = lambda w: w.reshape((-1,) + w.shape[2:])

    xa, xb = moe_gather(packed, src_tok, n_rows)
    h = ffn_up(xa, xb, flat(w1), flat(w3), tile_expert, n_valid, tm)
    y = ffn_down(h, flat(w2), tile_expert, n_valid, tm)
    pad = -t % COMBINE_TM
    pos = jnp.pad(dest.reshape(t, 2), ((0, pad), (0, 0)))
    return moe_combine(y, pos[:, 0], pos[:, 1], jnp.pad(route, ((0, pad), (0, 0))))


def kernel(x_prompt, x_sample, cache_moba_k, cache_moba_v, cache_sb_k, cache_sb_v, state_pool, state_conv,
           page_table, c_prompt, c_sample, w_ada, b_ada, w_in, pool_w, pool_scale, pool_proj, moba_proj,
           conv_w, conv_b, conv_ln_g, conv_ln_b, conv_pw, sb_proj, w_out, ln1_g, ln1_b, ln2_g, ln2_b,
           ffn_w1, ffn_w3, ffn_w2, moe_router, moe_router_b, moe_w1, moe_w3, moe_w2):
    bp, lp, _ = x_prompt.shape
    bs, ls, _ = x_sample.shape
    mp, ms = bp * lp, bs * ls
    past_len = page_table.shape[1] * PAGE_SIZE
    assert past_len // MOBA_BLOCK >= MOBA_TOPK and ls <= Q_PAD and lp % MOBA_BLOCK == 0

    c_all = jnp.zeros((16, D_MODEL), F32).at[:bp].set(c_prompt).at[bp:bp + bs].set(c_sample)
    mod = ada_mod(c_all, w_ada, b_ada).reshape(DEPTH, 16, 6, D_MODEL)

    def mods(l, k):
        mp_ = mod[l, :bp, k].reshape(bp, 1, D_MODEL)
        ms_ = jnp.repeat(mod[l, bp:bp + bs, k], ls, axis=0).reshape(1, ms, D_MODEL)
        return mp_, ms_

    pos_p = jnp.arange(lp, dtype=I32)
    pos_s = past_len + jnp.arange(ls, dtype=I32)
    tab_p, tab_s = rope_tables(pos_p), rope_tables(pos_s)

    xp = x_prompt.reshape(mp, D_MODEL)
    xs = x_sample.reshape(ms, D_MODEL)
    sc1, sh1 = mods(0, 1), mods(0, 0)
    up = ln_mod(xp, sc1[0], sh1[0], lp)
    us = ln_mod(xs, sc1[1], sh1[1], ls)
    zero_pool = jnp.zeros((bp, POOL_HALO, BRANCH_DIM), F32)
    zero_conv = jnp.zeros((bp, CONV_HALO, BRANCH_DIM), F32)
    outs_p = [[] for _ in range(6)]
    outs_s = [[] for _ in range(6)]

    def head4(t, b, l):
        return t.reshape(b, l, N_HEADS, HEAD_DIM)

    for l in range(DEPTH):
        g1p, g1s = mods(l, 2)
        sc2, sh2 = mods(l, 4), mods(l, 3)
        g2p, g2s = mods(l, 5)
        projs = (pool_proj, moba_proj, conv_pw, sb_proj)

        p3 = matmul(up, w_in, l, n=N_PROJ).reshape(bp, lp, N_PROJ)
        ha = pool_branch(p3, zero_pool, pool_w[l], pool_scale[l], pos0=0)
        hc, glu = conv_branch(p3, zero_conv, conv_w[l], conv_b[l], conv_ln_g[l], conv_ln_b[l])
        q_rot, k_rot, k_mean = rope_qk(p3, tab_p, with_means=True)
        hb = moba_prompt(q_rot, k_rot, p3, k_mean)
        hd = sb_prompt(p3)
        flat_p = lambda t: t.reshape(mp, BRANCH_DIM)
        mix = gate_mix(up, (flat_p(ha), flat_p(hb), flat_p(hc), flat_p(hd)), w_in, projs, l)
        attn = matmul(mix, w_out, l)
        outs_p[0].append(head4(k_rot, bp, lp))
        outs_p[1].append(head4(p3[:, :, 3 * BRANCH_DIM:4 * BRANCH_DIM], bp, lp))
        outs_p[2].append(head4(p3[:, :, 7 * BRANCH_DIM:8 * BRANCH_DIM], bp, lp))
        outs_p[3].append(head4(p3[:, :, 8 * BRANCH_DIM:9 * BRANCH_DIM], bp, lp))
        outs_p[4].append(p3[:, lp - POOL_BUF:, :BRANCH_DIM])
        outs_p[5].append(glu[:, lp - CONV_BUF:])

        ps3 = matmul(us, w_in, l, n=N_PROJ).reshape(bs, ls, N_PROJ)
        pool_hist = jnp.concatenate([state_pool[l], ps3[:, :, :BRANCH_DIM]], axis=1)
        pool_init = jnp.pad(state_pool[l], ((0, 0), (POOL_HALO - POOL_BUF, 0), (0, 0)))
        has = pool_branch(ps3, pool_init, pool_w[l], pool_scale[l], pos0=past_len)
        conv_init = jnp.pad(state_conv[l], ((0, 0), (CONV_HALO - CONV_BUF, 0), (0, 0)))
        hcs, glu_s = conv_branch(ps3, conv_init, conv_w[l], conv_b[l], conv_ln_g[l], conv_ln_b[l])
        conv_hist = jnp.concatenate([state_conv[l], glu_s], axis=1)
        qs_rot, ks_rot = rope_qk(ps3, tab_s, with_means=False)
        pad_q = lambda t: jnp.pad(t, ((0, 0), (0, Q_PAD - ls), (0, 0)))
        pad_kv = lambda t: jnp.pad(t, ((0, 0), (0, PAGE_SIZE - ls), (0, 0)))
        mv_s = ps3[:, :, 3 * BRANCH_DIM:4 * BRANCH_DIM]
        sq_s, sk_s, sv_s = (ps3[:, :, k * BRANCH_DIM:(k + 1) * BRANCH_DIM] for k in (6, 7, 8))
        k_mean_s = page_block_means(cache_moba_k, l, page_table)
        picks = moba_select(pad_q(qs_rot), k_mean_s)
        ids = picks[:, :ls, :N_HEADS * MOBA_TOPK].reshape(-1)
        hbs = moba_sample(ids, page_table, pad_q(qs_rot), pad_kv(ks_rot), pad_kv(mv_s),
                          cache_moba_k, cache_moba_v, l, n_q=ls, past_len=past_len)
        hds = sb_sample(page_table, pad_q(sq_s), pad_kv(sk_s), pad_kv(sv_s), cache_sb_k, cache_sb_v, l,
                        past_len=past_len)[:, :ls].reshape(ms, BRANCH_DIM)
        flat_s = lambda t: t.reshape(ms, BRANCH_DIM)
        mix_s = gate_mix(us, (flat_s(has), hbs, flat_s(hcs), hds), w_in, projs, l)
        attn_s = matmul(mix_s, w_out, l)
        outs_s[0].append(head4(ks_rot, bs, ls))
        outs_s[1].append(head4(mv_s, bs, ls))
        outs_s[2].append(head4(sk_s, bs, ls))
        outs_s[3].append(head4(sv_s, bs, ls))
        outs_s[4].append(pool_hist[:, -POOL_BUF:])
        outs_s[5].append(conv_hist[:, -CONV_BUF:])

        j = l // 2
        if l % 2 == 0:
            xp, u2p = resid_ln(xp, attn, g1p, ln1_g[l], ln1_b[l], lp, sc2[0], sh2[0])
            xs, u2s = resid_ln(xs, attn_s, g1s, ln1_g[l], ln1_b[l], ls, sc2[1], sh2[1])
            fp = dense_ffn(u2p, ffn_w1, ffn_w3, ffn_w2, j)
            fs = dense_ffn(u2s, ffn_w1, ffn_w3, ffn_w2, j)
        else:
            router = (moe_router[j], moe_router_b[j])
            xp, pk_p, rt_p = resid_ln(xp, attn, g1p, ln1_g[l], ln1_b[l], lp, sc2[0], sh2[0], router)
            xs, pk_s, rt_s = resid_ln(xs, attn_s, g1s, ln1_g[l], ln1_b[l], ls, sc2[1], sh2[1], router)
            f_all = moe_ffn(jnp.concatenate([pk_p, pk_s], axis=0), jnp.concatenate([rt_p, rt_s], axis=0),
                            moe_w1, moe_w3, moe_w2, j)
            fp, fs = f_all, f_all
        fs_row0 = 0 if l % 2 == 0 else mp
        if l + 1 < DEPTH:
            nsc, nsh = mods(l + 1, 1), mods(l + 1, 0)
            xp, up = resid_ln(xp, fp, g2p, ln2_g[l], ln2_b[l], lp, nsc[0], nsh[0])
            xs, us = resid_ln(xs, fs, g2s, ln2_g[l], ln2_b[l], ls, nsc[1], nsh[1], y_row0=fs_row0)
        else:
            xp = resid_ln(xp, fp, g2p, ln2_g[l], ln2_b[l], lp)
            xs = resid_ln(xs, fs, g2s, ln2_g[l], ln2_b[l], ls, y_row0=fs_row0)

    stack = lambda group: tuple(jnp.stack(t) for t in group)
    return (xp.reshape(bp, lp, D_MODEL), xs.reshape(bs, ls, D_MODEL)) + stack(outs_p) + stack(outs_s)
```

```python
import functools
import math

import jax
import jax.numpy as jnp
from jax import lax
from jax.experimental import pallas as pl
from jax.experimental.pallas import tpu as pltpu

F32 = jnp.float32
BF16 = jnp.bfloat16
I32 = jnp.int32

D_MODEL = 2048
DEPTH = 4
PAGE_SIZE = 128
HEAD_DIM = 128
BRANCH_DIM = 512
N_HEADS = BRANCH_DIM // HEAD_DIM
POOL_WINDOWS = (2, 4, 8, 16)
POOL_BUF = 15
POOL_HALO = 16
MOBA_BLOCK = 256
MOBA_TOPK = 3
ROPE_THETA = 500000.0
ROPE_DIM = HEAD_DIM // 4
CONV_WIDTH = 31
CONV_BUF = CONV_WIDTH - 1
CONV_HALO = 32
N_BRANCH = 4
D_FF = 5632
N_EXPERTS = 8
ALPHA = (2.0 * DEPTH) ** 0.25
LN_EPS = 1e-5
N_PROJ = 9 * BRANCH_DIM
ATT_SCALE = HEAD_DIM ** -0.5

SB_EXIT = -104.0
NEG_BIG = -1e30

VMEM_LIMIT_V7X = 56 * 1024 * 1024
LANES = 128


def _params(n_axes):
    return pltpu.CompilerParams(dimension_semantics=("arbitrary",) * n_axes,
                                vmem_limit_bytes=VMEM_LIMIT_V7X)


def _ln(x):
    mu = jnp.mean(x, axis=-1, keepdims=True)
    xc = x - mu
    var = jnp.mean(xc * xc, axis=-1, keepdims=True)
    return xc * lax.rsqrt(var + LN_EPS)


def _dot(a, b):
    return jnp.dot(a, b, preferred_element_type=F32)


def _dot_nt(a, b):
    return lax.dot_general(a, b, (((1,), (1,)), ((), ())), preferred_element_type=F32)


def _split2(x):
    hi = x.astype(BF16)
    lo = (x - hi.astype(F32)).astype(BF16)
    return hi, lo


def _dot_nt_precise(a, b):
    ah, al = _split2(a)
    bh, bl = _split2(b)
    return _dot_nt(ah, bh) + (_dot_nt(ah, bl) + _dot_nt(al, bh))


def _dot_exact_rhs(a, b_bf16):
    hi, lo = _split2(a)
    return _dot(hi, b_bf16) + _dot(lo, b_bf16)


def _softplus_neg_abs(z):
    return jnp.log1p(jnp.exp(-jnp.abs(z)))


def _ada_kernel(c_ref, w_ref, b_ref, o_ref):
    c = c_ref[...]
    s = (c * jax.nn.sigmoid(c)).astype(BF16)
    o_ref[...] = _dot(s, w_ref[...].astype(BF16)) + b_ref[...]


def ada_mod(c_all, w_ada, b_ada):
    rows = c_all.shape[0]
    tn = 1024
    n = w_ada.shape[-1]
    return pl.pallas_call(
        _ada_kernel,
        grid=(DEPTH, n // tn),
        in_specs=[pl.BlockSpec((rows, D_MODEL), lambda l, j: (0, 0)),
                  pl.BlockSpec((None, D_MODEL, tn), lambda l, j: (l, 0, j)),
                  pl.BlockSpec((None, 1, tn), lambda l, j: (l, 0, j))],
        out_specs=pl.BlockSpec((None, rows, tn), lambda l, j: (l, 0, j)),
        out_shape=jax.ShapeDtypeStruct((DEPTH, rows, n), F32),
        compiler_params=_params(2),
        name="ada_mod",
    )(c_all, w_ada, b_ada.reshape(DEPTH, 1, n))


def _mod_spec(mod, tm, tiles_per_group):
    if mod.shape[1] == 1:
        return pl.BlockSpec((None, 1, D_MODEL), lambda i: (i // tiles_per_group, 0, 0))
    return pl.BlockSpec((None, tm, D_MODEL), lambda i: (0, i, 0))


def _ln_mod_kernel(x_ref, sc_ref, sh_ref, o_ref):
    o_ref[...] = (_ln(x_ref[...]) * (1.0 + sc_ref[...]) + sh_ref[...]).astype(o_ref.dtype)


def ln_mod(x, sc, sh, rows_per_group):
    m = x.shape[0]
    tm = min(m, 256)
    tpg = max(rows_per_group // tm, 1)
    row = pl.BlockSpec((tm, D_MODEL), lambda i: (i, 0))
    return pl.pallas_call(
        _ln_mod_kernel,
        grid=(m // tm,),
        in_specs=[row, _mod_spec(sc, tm, tpg), _mod_spec(sh, tm, tpg)],
        out_specs=row,
        out_shape=jax.ShapeDtypeStruct((m, D_MODEL), BF16),
        compiler_params=_params(1),
        name="ln_mod",
    )(x, sc, sh)


def _pack_bf16_pairs(u):
    half = u.shape[1] // 2
    a = pltpu.bitcast(u[:, :half].astype(BF16).astype(F32), jnp.uint32)
    b = pltpu.bitcast(u[:, half:].astype(BF16).astype(F32), jnp.uint32)
    return (a & jnp.uint32(0xFFFF0000)) | (b >> 16)


def _route_top2(u, rh_ref, rl_ref, rb_ref):
    uh, ul = _split2(u)
    lg = _dot(uh, rh_ref[...]) + (_dot(uh, rl_ref[...]) + _dot(ul, rh_ref[...])) + rb_ref[...]
    lane = lax.broadcasted_iota(I32, lg.shape, 1)
    m1 = jnp.max(lg, axis=-1, keepdims=True)
    i1 = jnp.min(jnp.where(lg == m1, lane, LANES), axis=-1, keepdims=True)
    lg2 = jnp.where(lane == i1, -jnp.inf, lg)
    m2 = jnp.max(lg2, axis=-1, keepdims=True)
    i2 = jnp.min(jnp.where(lg2 == m2, lane, LANES), axis=-1, keepdims=True)
    e2 = jnp.exp(m2 - m1)
    g1 = 1.0 / (1.0 + e2)
    g2 = e2 / (1.0 + e2)
    out = jnp.where(lane == 0, i1.astype(F32), 0.0)
    out = jnp.where(lane == 1, i2.astype(F32), out)
    out = jnp.where(lane == 2, g1, out)
    out = jnp.where(lane == 3, g2, out)
    return out


def _resid_ln_kernel(*refs, mode):
    x_ref, y_ref, g_ref, lg_ref, lb_ref = refs[:5]
    t = ALPHA * x_ref[...] + g_ref[...] * y_ref[...]
    xn = _ln(t) * lg_ref[...] + lb_ref[...]
    if mode == "last":
        refs[5][...] = xn
        return
    sc_ref, sh_ref = refs[5:7]
    u = _ln(xn) * (1.0 + sc_ref[...]) + sh_ref[...]
    if mode == "next":
        xo_ref, uo_ref = refs[7:9]
        xo_ref[...] = xn
        uo_ref[...] = u.astype(BF16)
    else:
        rh_ref, rl_ref, rb_ref, xo_ref, up_ref, rt_ref = refs[7:13]
        xo_ref[...] = xn
        up_ref[...] = _pack_bf16_pairs(u)
        rt_ref[...] = _route_top2(u, rh_ref, rl_ref, rb_ref)


def resid_ln(x, y, g, ln_g, ln_b, rows_per_group, sc=None, sh=None, router=None, y_row0=0):
    m = x.shape[0]
    tm = min(m, 512)
    tpg = max(rows_per_group // tm, 1)
    assert y_row0 % tm == 0
    row = pl.BlockSpec((tm, D_MODEL), lambda i: (i, 0))
    vec = pl.BlockSpec((1, D_MODEL), lambda i: (0, 0))
    ins = [x, y, g, ln_g.reshape(1, D_MODEL), ln_b.reshape(1, D_MODEL)]
    in_specs = [row, pl.BlockSpec((tm, D_MODEL), lambda i: (y_row0 // tm + i, 0)), _mod_spec(g, tm, tpg), vec, vec]
    x_out = jax.ShapeDtypeStruct((m, D_MODEL), F32)
    if sc is None:
        mode, out_shape, out_specs = "last", x_out, row
    else:
        ins += [sc, sh]
        in_specs += [_mod_spec(sc, tm, tpg), _mod_spec(sh, tm, tpg)]
        if router is None:
            mode = "next"
            out_shape = (x_out, jax.ShapeDtypeStruct((m, D_MODEL), BF16))
            out_specs = (row, row)
        else:
            mode = "route"
            rw, rb = router
            rw_pad = jnp.zeros((D_MODEL, LANES), F32).at[:, :N_EXPERTS].set(rw)
            rhi = rw_pad.astype(BF16)
            rlo = (rw_pad - rhi.astype(F32)).astype(BF16)
            rb_pad = jnp.full((1, LANES), -jnp.inf, F32).at[0, :N_EXPERTS].set(rb)
            ins += [rhi, rlo, rb_pad]
            full = pl.BlockSpec((D_MODEL, LANES), lambda i: (0, 0))
            in_specs += [full, full, pl.BlockSpec((1, LANES), lambda i: (0, 0))]
            out_shape = (x_out, jax.ShapeDtypeStruct((m, D_MODEL // 2), jnp.uint32),
                         jax.ShapeDtypeStruct((m, LANES), F32))
            out_specs = (row, pl.BlockSpec((tm, D_MODEL // 2), lambda i: (i, 0)),
                         pl.BlockSpec((tm, LANES), lambda i: (i, 0)))
    return pl.pallas_call(
        functools.partial(_resid_ln_kernel, mode=mode),
        grid=(m // tm,),
        in_specs=in_specs, out_specs=out_specs, out_shape=out_shape,
        compiler_params=_params(1),
        name="resid_ln_" + mode,
    )(*ins)


def _mm_kernel(x_ref, w_ref, o_ref, wb_ref):
    @pl.when(pl.program_id(1) == 0)
    def _():
        wb_ref[...] = w_ref[...].astype(BF16)
    o_ref[...] = _dot(x_ref[...], wb_ref[...]).astype(o_ref.dtype)


def matmul(x, w, layer, *, col0=0, n=None, tn=512, tm=1024, out_dtype=F32):
    m, k = x.shape
    n = w.shape[-1] if n is None else n
    tm = min(m, tm)
    cb = col0 // tn
    return pl.pallas_call(
        _mm_kernel,
        grid=(n // tn, m // tm),
        in_specs=[pl.BlockSpec((tm, k), lambda j, i: (i, 0)),
                  pl.BlockSpec((None, k, tn), lambda j, i: (layer, 0, cb + j))],
        out_specs=pl.BlockSpec((tm, tn), lambda j, i: (i, j)),
        out_shape=jax.ShapeDtypeStruct((m, n), out_dtype),
        scratch_shapes=[pltpu.VMEM((k, tn), BF16)],
        compiler_params=_params(2),
        name="matmul",
    )(x, w)


def _pool_kernel(x_ref, init_ref, wg_ref, scale_ref, o_ref, halo_ref, ext_ref, *, tm, n_tiles, pos0):
    i = pl.program_id(1)

    @pl.when(i == 0)
    def _():
        halo_ref[...] = init_ref[...]

    x = x_ref[...]
    ext_ref[0:POOL_HALO, :] = halo_ref[...]
    ext_ref[POOL_HALO:POOL_HALO + tm, :] = x
    if n_tiles > 1:
        halo_ref[...] = x[tm - POOL_HALO:, :]
    pos = pos0 + i * tm + lax.broadcasted_iota(I32, (tm, 1), 0)
    for g, w in enumerate(POOL_WINDOWS):
        cols = pl.ds(g * LANES, LANES)
        s = ext_ref[pl.ds(POOL_HALO, tm), cols]
        for j in range(1, w):
            s = s + ext_ref[pl.ds(POOL_HALO - j, tm), cols]
        cnt = jnp.minimum(pos + 1, w).astype(F32)
        d = s / cnt - x[:, g * LANES:(g + 1) * LANES]
        y = _dot(d.astype(BF16), wg_ref[g].astype(BF16))
        o_ref[:, g * LANES:(g + 1) * LANES] = (y * scale_ref[:, g * LANES:(g + 1) * LANES]).astype(o_ref.dtype)


def pool_branch(proj, init, w_grp, scale, *, pos0):
    n_seq, seq_len, _ = proj.shape
    tm = min(seq_len, 256)
    n_tiles = seq_len // tm
    return pl.pallas_call(
        functools.partial(_pool_kernel, tm=tm, n_tiles=n_tiles, pos0=pos0),
        grid=(n_seq, n_tiles),
        in_specs=[pl.BlockSpec((None, tm, BRANCH_DIM), lambda b, i: (b, i, 0)),
                  pl.BlockSpec((None, POOL_HALO, BRANCH_DIM), lambda b, i: (b, 0, 0)),
                  pl.BlockSpec((len(POOL_WINDOWS), LANES, LANES), lambda b, i: (0, 0, 0)),
                  pl.BlockSpec((1, BRANCH_DIM), lambda b, i: (0, 0))],
        out_specs=pl.BlockSpec((None, tm, BRANCH_DIM), lambda b, i: (b, i, 0)),
        out_shape=jax.ShapeDtypeStruct((n_seq, seq_len, BRANCH_DIM), BF16),
        scratch_shapes=[pltpu.VMEM((POOL_HALO, BRANCH_DIM), F32),
                        pltpu.VMEM((POOL_HALO + tm, BRANCH_DIM), F32)],
        compiler_params=_params(2),
        name="pool_branch",
    )(proj, init, w_grp, scale.reshape(1, BRANCH_DIM))


def _conv_kernel(a_ref, init_ref, w_ref, b_ref, g_ref, beta_ref, o_ref, h_ref,
                 halo_ref, ext_ref, y_ref, *, tm, n_tiles, rc):
    i = pl.program_id(1)

    @pl.when(i == 0)
    def _():
        halo_ref[...] = init_ref[...]

    a = a_ref[...]
    h = a[:, :BRANCH_DIM] * jax.nn.sigmoid(a[:, BRANCH_DIM:])
    h_ref[...] = h
    ext_ref[0:CONV_HALO, :] = halo_ref[...]
    ext_ref[CONV_HALO:CONV_HALO + tm, :] = h
    if n_tiles > 1:
        halo_ref[...] = h[tm - CONV_HALO:, :]
    off = CONV_HALO - CONV_BUF
    for c in range(BRANCH_DIM // LANES):
        cols = pl.ds(c * LANES, LANES)
        for r0 in range(0, tm, rc):
            acc = jnp.zeros((rc, LANES), F32) + b_ref[:, c * LANES:(c + 1) * LANES]
            for k in range(CONV_WIDTH):
                acc = acc + ext_ref[pl.ds(r0 + off + k, rc), cols] * w_ref[k:k + 1, c * LANES:(c + 1) * LANES]
            y_ref[pl.ds(r0, rc), cols] = acc
    y = _ln(y_ref[...]) * g_ref[...] + beta_ref[...]
    o_ref[...] = (y * jax.nn.sigmoid(y)).astype(o_ref.dtype)


def conv_branch(proj, init, w, b, g, beta):
    n_seq, seq_len, _ = proj.shape
    tm = min(seq_len, 128)
    rc = min(tm, 64)
    n_tiles = seq_len // tm
    col_block = (4 * BRANCH_DIM) // (2 * BRANCH_DIM)
    w_pad = jnp.zeros((CONV_HALO, BRANCH_DIM), F32).at[:CONV_WIDTH].set(w)
    vec = pl.BlockSpec((1, BRANCH_DIM), lambda bb, i: (0, 0))
    row = pl.BlockSpec((None, tm, BRANCH_DIM), lambda bb, i: (bb, i, 0))
    return pl.pallas_call(
        functools.partial(_conv_kernel, tm=tm, n_tiles=n_tiles, rc=rc),
        grid=(n_seq, n_tiles),
        in_specs=[pl.BlockSpec((None, tm, 2 * BRANCH_DIM), lambda bb, i: (bb, i, col_block)),
                  pl.BlockSpec((None, CONV_HALO, BRANCH_DIM), lambda bb, i: (bb, 0, 0)),
                  pl.BlockSpec((CONV_HALO, BRANCH_DIM), lambda bb, i: (0, 0)),
                  vec, vec, vec],
        out_specs=(row, row),
        out_shape=(jax.ShapeDtypeStruct((n_seq, seq_len, BRANCH_DIM), BF16),
                   jax.ShapeDtypeStruct((n_seq, seq_len, BRANCH_DIM), F32)),
        scratch_shapes=[pltpu.VMEM((CONV_HALO, BRANCH_DIM), F32),
                        pltpu.VMEM((CONV_HALO + tm, BRANCH_DIM), F32),
                        pltpu.VMEM((tm, BRANCH_DIM), F32)],
        compiler_params=_params(2),
        name="conv_branch",
    )(proj, init, w_pad, b.reshape(1, -1), g.reshape(1, -1), beta.reshape(1, -1))


def rope_tables(pos):
    half = ROPE_DIM // 2
    inv_freq = ROPE_THETA ** (-jnp.arange(half, dtype=F32) / half)
    ang = pos.astype(F32)[:, None] * inv_freq[None, :]
    cos, sin = jnp.cos(ang), jnp.sin(ang)
    n = pos.shape[0]
    z = jnp.zeros((n, half), F32)
    rest = HEAD_DIM - ROPE_DIM
    cos_t = jnp.concatenate([cos, cos, jnp.ones((n, rest), F32)], axis=1)
    sin_up = jnp.concatenate([-sin, z, jnp.zeros((n, rest), F32)], axis=1)
    sin_dn = jnp.concatenate([z, sin, jnp.zeros((n, rest), F32)], axis=1)
    return cos_t, sin_up, sin_dn


def _rope(x, cos_t, sin_up, sin_dn):
    half = ROPE_DIM // 2
    width = x.shape[1]
    tile = lambda t: jnp.concatenate([t] * (width // HEAD_DIM), axis=1)
    up = pltpu.roll(x, width - half, axis=1)
    dn = pltpu.roll(x, half, axis=1)
    return x * tile(cos_t) + up * tile(sin_up) + dn * tile(sin_dn)


def _rope_kernel(q_ref, k_ref, cos_ref, su_ref, sd_ref, qo_ref, ko_ref, *maybe_km_ref, tm):
    cos_t, su, sd = cos_ref[...], su_ref[...], sd_ref[...]
    qo_ref[...] = _rope(q_ref[...], cos_t, su, sd)
    kr = _rope(k_ref[...], cos_t, su, sd)
    ko_ref[...] = kr
    if maybe_km_ref:
        maybe_km_ref[0][...] = jnp.sum(kr, axis=0, keepdims=True) * (1.0 / tm)


def rope_qk(proj, tables, *, with_means):
    n_seq, seq_len, _ = proj.shape
    tm = min(seq_len, MOBA_BLOCK)
    n_tiles = seq_len // tm
    row = pl.BlockSpec((None, tm, BRANCH_DIM), lambda b, i: (b, i, 0))
    tab = pl.BlockSpec((tm, HEAD_DIM), lambda b, i: (i, 0))
    out_shape = [jax.ShapeDtypeStruct((n_seq, seq_len, BRANCH_DIM), F32)] * 2
    out_specs = [row, row]
    if with_means:
        assert tm == MOBA_BLOCK
        out_shape.append(jax.ShapeDtypeStruct((n_seq, n_tiles, 1, BRANCH_DIM), F32))
        out_specs.append(pl.BlockSpec((None, None, 1, BRANCH_DIM), lambda b, i: (b, i, 0, 0)))
    return pl.pallas_call(
        functools.partial(_rope_kernel, tm=tm),
        grid=(n_seq, n_tiles),
        in_specs=[pl.BlockSpec((None, tm, BRANCH_DIM), lambda b, i: (b, i, 1)),
                  pl.BlockSpec((None, tm, BRANCH_DIM), lambda b, i: (b, i, 2)),
                  tab, tab, tab],
        out_specs=tuple(out_specs), out_shape=tuple(out_shape),
        compiler_params=_params(2),
        name="rope_qk",
    )(proj, proj, *tables)


def _moba_prompt_kernel(q_ref, k_ref, v_ref, km_ref, o_ref, *, n_blk, chunk):
    blk = MOBA_BLOCK
    bpc = chunk // blk
    qi = pl.program_id(2)
    qf = q_ref[...]
    st = _dot_nt_precise(km_ref[...], qf)
    n_row = lax.broadcasted_iota(I32, (n_blk, 1), 0)
    valid = n_row < qi
    sm = jnp.where(valid, st, -jnp.inf)
    rank = jnp.zeros(st.shape, I32)
    for m in range(n_blk):
        other = sm[m:m + 1, :]
        beats = (other > sm) | ((other == sm) & (m < n_row))
        rank = rank + jnp.where(beats & (m < qi), 1, 0)
    chosen = (valid & (rank < MOBA_TOPK)) | (n_row == qi)
    bias_t = jnp.where(chosen, 0.0, NEG_BIG).astype(BF16)

    qb = qf.astype(BF16)
    qpos = qi * blk + lax.broadcasted_iota(I32, (blk, 1), 0)
    krel = lax.broadcasted_iota(I32, (1, chunk), 1)

    def visit(c, carry, causal):
        m_i, l_i, acc = carry
        k0 = pl.multiple_of(c * chunk, chunk)
        kb = k_ref[pl.ds(k0, chunk), :].astype(BF16)
        vb = v_ref[pl.ds(k0, chunk), :].astype(BF16)
        widen = jnp.where(n_row == c * bpc + krel // blk, 1.0, 0.0).astype(BF16)
        bias = lax.dot_general(bias_t, widen, (((0,), (0,)), ((), ())), preferred_element_type=F32)
        lg = _dot_nt(qb, kb) * ATT_SCALE + bias
        if causal:
            lg = jnp.where(k0 + krel <= qpos, lg, NEG_BIG)
        m_new = jnp.maximum(m_i, jnp.max(lg, axis=-1, keepdims=True))
        alpha = jnp.exp(m_i - m_new)
        p = jnp.exp(lg - m_new)
        l_new = alpha * l_i + jnp.sum(p, axis=-1, keepdims=True)
        acc = alpha * acc + _dot(p.astype(BF16), vb)
        return m_new, l_new, acc

    init = (jnp.full((blk, 1), NEG_BIG, F32), jnp.zeros((blk, 1), F32), jnp.zeros((blk, HEAD_DIM), F32))
    own_chunk = qi // bpc
    carry = lax.fori_loop(0, own_chunk, lambda c, cr: visit(c, cr, False), init)
    _, l_f, acc = visit(own_chunk, carry, True)
    o_ref[...] = (acc / l_f).astype(o_ref.dtype)


def moba_prompt(q_rot, k_rot, proj, k_mean):
    n_seq, seq_len, _ = proj.shape
    n_blk = seq_len // MOBA_BLOCK
    chunk = min(seq_len, 4 * MOBA_BLOCK)
    assert seq_len % chunk == 0
    km = k_mean.reshape(n_seq, n_blk, BRANCH_DIM)
    tile = pl.BlockSpec((None, MOBA_BLOCK, HEAD_DIM), lambda b, h, i: (b, i, h))
    return pl.pallas_call(
        functools.partial(_moba_prompt_kernel, n_blk=n_blk, chunk=chunk),
        grid=(n_seq, N_HEADS, n_blk),
        in_specs=[tile,
                  pl.BlockSpec((None, seq_len, HEAD_DIM), lambda b, h, i: (b, 0, h)),
                  pl.BlockSpec((None, seq_len, HEAD_DIM), lambda b, h, i: (b, 0, 3 * N_HEADS + h)),
                  pl.BlockSpec((None, n_blk, HEAD_DIM), lambda b, h, i: (b, 0, h))],
        out_specs=tile,
        out_shape=jax.ShapeDtypeStruct((n_seq, seq_len, BRANCH_DIM), BF16),
        compiler_params=_params(3),
        name="moba_prompt",
    )(q_rot, k_rot, proj, km)


def _strict_lower(n):
    r = lax.broadcasted_iota(I32, (n, n), 0)
    c = lax.broadcasted_iota(I32, (n, n), 1)
    return jnp.where(r > c, 1.0, 0.0).astype(BF16)


def _sb_block(qb, kb, vb, tri, allowed, carry, acc):
    z = _dot_nt(qb, kb) * ATT_SCALE
    sp = _softplus_neg_abs(z)
    log_sig = jnp.minimum(z, 0.0) - sp
    log_1m = jnp.where(allowed, -jnp.maximum(z, 0.0) - sp, 0.0)
    later = _dot_exact_rhs(log_1m, tri) + carry
    a = jnp.where(allowed, jnp.exp(log_sig + later), 0.0)
    acc = acc + _dot(a.astype(BF16), vb)
    carry = carry + jnp.sum(log_1m, axis=-1, keepdims=True)
    return carry, acc


def _sb_prompt_kernel(q_ref, k_ref, v_ref, tri_ref, o_ref, *, tq, tk, back):
    qi = pl.program_id(1)
    win = tq + back
    qpos = qi * tq + lax.broadcasted_iota(I32, (tq, 1), 0)
    heads = [slice(h * HEAD_DIM, (h + 1) * HEAD_DIM) for h in range(N_HEADS)]
    qbs = [q_ref[:, cols].astype(BF16) for cols in heads]

    w0 = pl.multiple_of(jnp.maximum(qi * tq - back, 0), tk)
    allowed = (w0 + lax.broadcasted_iota(I32, (1, win), 1)) < qpos
    tri_win = tri_ref[...]
    carries, accs = [], []
    live = jnp.float32(SB_EXIT)
    for h, cols in enumerate(heads):
        kb = k_ref[pl.ds(w0, win), cols].astype(BF16)
        vb = v_ref[pl.ds(w0, win), cols].astype(BF16)
        carry, acc = _sb_block(qbs[h], kb, vb, tri_win, allowed,
                               jnp.zeros((tq, 1), F32), jnp.zeros((tq, HEAD_DIM), F32))
        carries.append(carry)
        accs.append(acc)
        live = jnp.maximum(live, jnp.max(carry))

    krel = lax.broadcasted_iota(I32, (1, tk), 1)
    tri_blk = tri_ref[0:tk, 0:tk]

    def cond(c):
        return (c[0] >= 0) & (c[1] > 0)

    def body(c):
        j, _, carries, accs = c
        k0 = pl.multiple_of(j * tk, tk)
        allowed = (k0 + krel) < qpos
        new_carries, new_accs = [], []
        live = jnp.float32(SB_EXIT)
        for h, cols in enumerate(heads):
            kb = k_ref[pl.ds(k0, tk), cols].astype(BF16)
            vb = v_ref[pl.ds(k0, tk), cols].astype(BF16)
            carry, acc = _sb_block(qbs[h], kb, vb, tri_blk, allowed, carries[h], accs[h])
            new_carries.append(carry)
            new_accs.append(acc)
            live = jnp.maximum(live, jnp.max(carry))
        return j - 1, (live > SB_EXIT).astype(I32), tuple(new_carries), tuple(new_accs)

    init = (w0 // tk - 1, (live > SB_EXIT).astype(I32), tuple(carries), tuple(accs))
    accs = lax.while_loop(cond, body, init)[3]
    for cols, acc in zip(heads, accs):
        o_ref[:, cols] = acc.astype(o_ref.dtype)


def sb_prompt(proj):
    n_seq, seq_len, _ = proj.shape
    tq, tk, back = 256, 128, 256
    win = tq + back
    assert seq_len >= win and seq_len % tq == 0
    idx = jnp.arange(win, dtype=I32)
    tri = (idx[:, None] > idx[None, :]).astype(BF16)
    seq = lambda blk: pl.BlockSpec((None, seq_len, BRANCH_DIM), lambda b, i: (b, 0, blk))
    return pl.pallas_call(
        functools.partial(_sb_prompt_kernel, tq=tq, tk=tk, back=back),
        grid=(n_seq, seq_len // tq),
        in_specs=[pl.BlockSpec((None, tq, BRANCH_DIM), lambda b, i: (b, i, 6)), seq(7), seq(8),
                  pl.BlockSpec((win, win), lambda b, i: (0, 0))],
        out_specs=pl.BlockSpec((None, tq, BRANCH_DIM), lambda b, i: (b, i, 0)),
        out_shape=jax.ShapeDtypeStruct((n_seq, seq_len, BRANCH_DIM), BF16),
        compiler_params=_params(2),
        name="sb_prompt",
    )(proj, proj, proj, tri)


PAGES_PER_STEP = 8
MEAN_PAGES_PER_STEP = 16
Q_PAD = 8
ROWS_PER_VREG = 8
PAGE_ROWS = PAGE_SIZE * N_HEADS
STACK = N_HEADS * Q_PAD


def _paged(cache):
    return cache.reshape(cache.shape[0], cache.shape[1], PAGE_ROWS // ROWS_PER_VREG, ROWS_PER_VREG, HEAD_DIM)


_PAGE_BLOCK = (None, None, PAGE_ROWS // ROWS_PER_VREG, ROWS_PER_VREG, HEAD_DIM)


def _page_rows(ref):
    return ref[...].reshape(PAGE_ROWS, HEAD_DIM)


def _page_mean_kernel(pt_ref, *refs):
    o_ref = refs[-1]
    for n in range(MEAN_PAGES_PER_STEP // 2):
        s = jnp.sum(refs[2 * n][...], axis=0) + jnp.sum(refs[2 * n + 1][...], axis=0)
        mean = (s[:N_HEADS] + s[N_HEADS:]) * (1.0 / MOBA_BLOCK)
        for h in range(N_HEADS):
            o_ref[h, n:n + 1, :] = mean[h:h + 1, :]


def page_block_means(cache, layer, page_table):
    n_seq, n_pages = page_table.shape
    pps = MEAN_PAGES_PER_STEP
    specs = [pl.BlockSpec(_PAGE_BLOCK, functools.partial(lambda b, s, pt, k: (layer, pt[b, s * pps + k], 0, 0, 0), k=k))
             for k in range(pps)]
    return pl.pallas_call(
        _page_mean_kernel,
        grid_spec=pltpu.PrefetchScalarGridSpec(
            num_scalar_prefetch=1,
            grid=(n_seq, n_pages // pps),
            in_specs=specs,
            out_specs=pl.BlockSpec((None, N_HEADS, pps // 2, HEAD_DIM), lambda b, s, pt: (b, 0, s, 0))),
        out_shape=jax.ShapeDtypeStruct((n_seq, N_HEADS, n_pages // 2, HEAD_DIM), F32),
        compiler_params=_params(2),
        name="page_block_means",
    )(page_table, *([_paged(cache)] * pps))


def _moba_select_kernel(q_ref, km_ref, o_ref, *, n_blk):
    lane = lax.broadcasted_iota(I32, (Q_PAD, LANES), 1)
    n_iota = lax.broadcasted_iota(I32, (Q_PAD, n_blk), 1)
    out = jnp.zeros((Q_PAD, LANES), I32)
    for h in range(N_HEADS):
        cols = slice(h * HEAD_DIM, (h + 1) * HEAD_DIM)
        s = _dot_nt_precise(q_ref[:, cols], km_ref[h])
        for slot in range(MOBA_TOPK):
            top = jnp.max(s, axis=-1, keepdims=True)
            idx = jnp.min(jnp.where(s == top, n_iota, n_blk), axis=-1, keepdims=True)
            s = jnp.where(n_iota == idx, -jnp.inf, s)
            out = jnp.where(lane == h * MOBA_TOPK + slot, idx, out)
    o_ref[...] = out


def moba_select(q_pad, k_mean):
    n_seq, _, n_blk, _ = k_mean.shape
    return pl.pallas_call(
        functools.partial(_moba_select_kernel, n_blk=n_blk),
        grid=(n_seq,),
        in_specs=[pl.BlockSpec((None, Q_PAD, BRANCH_DIM), lambda b: (b, 0, 0)),
                  pl.BlockSpec((None, N_HEADS, n_blk, HEAD_DIM), lambda b: (b, 0, 0, 0))],
        out_specs=pl.BlockSpec((None, Q_PAD, LANES), lambda b: (b, 0, 0)),
        out_shape=jax.ShapeDtypeStruct((n_seq, Q_PAD, LANES), I32),
        compiler_params=_params(1),
        name="moba_select",
    )(q_pad, k_mean)


def _moba_sample_kernel(ids_ref, pt_ref, q_ref, kn_ref, vn_ref, *refs, past_len):
    n_pg = 2 * MOBA_TOPK
    k_refs, v_refs, o_ref = refs[:n_pg], refs[n_pg:2 * n_pg], refs[2 * n_pg]
    head = pl.program_id(1)
    qi = pl.program_id(2)
    qb = q_ref[...].astype(BF16)
    qpos = past_len + lax.broadcasted_iota(I32, (Q_PAD, 1), 0)
    krel = lax.broadcasted_iota(I32, (1, PAGE_SIZE), 1)
    own_mask = (past_len + krel) <= qpos
    mine = (lax.broadcasted_iota(I32, (1, PAGE_ROWS), 1) % N_HEADS) == head
    logits = [jnp.where(mine, _dot_nt(qb, _page_rows(r).astype(BF16)) * ATT_SCALE, NEG_BIG) for r in k_refs]
    own = jnp.where(own_mask, _dot_nt(qb, kn_ref[...].astype(BF16)) * ATT_SCALE, NEG_BIG)
    top = jnp.max(own, axis=-1, keepdims=True)
    for lg in logits:
        top = jnp.maximum(top, jnp.max(lg, axis=-1, keepdims=True))
    p_own = jnp.where(own_mask, jnp.exp(own - top), 0.0)
    den = jnp.sum(p_own, axis=-1, keepdims=True)
    acc = _dot(p_own.astype(BF16), vn_ref[...].astype(BF16))
    for lg, vr in zip(logits, v_refs):
        p = jnp.where(mine, jnp.exp(lg - top), 0.0)
        den = den + jnp.sum(p, axis=-1, keepdims=True)
        acc = acc + _dot(p.astype(BF16), _page_rows(vr).astype(BF16))
    out = acc / den
    row = lax.broadcasted_iota(I32, (Q_PAD, 1), 0)
    o_ref[...] = jnp.sum(jnp.where(row == qi, out, 0.0), axis=0, keepdims=True).astype(o_ref.dtype)


def moba_sample(ids, page_table, q_pad, k_new, v_new, k_cache, v_cache, layer, *, n_q, past_len):
    n_seq = q_pad.shape[0]
    kc, vc = _paged(k_cache), _paged(v_cache)

    def page_map(b, h, q, ids_ref, pt_ref, *, slot, half):
        blk = ids_ref[((b * n_q + q) * N_HEADS + h) * MOBA_TOPK + slot]
        return (layer, pt_ref[b, 2 * blk + half], 0, 0, 0)

    page_specs = [pl.BlockSpec(_PAGE_BLOCK, functools.partial(page_map, slot=s, half=hf))
                  for s in range(MOBA_TOPK) for hf in range(2)]
    head = lambda rows: pl.BlockSpec((None, rows, HEAD_DIM), lambda b, h, q, i_, p_: (b, 0, h))
    out = pl.pallas_call(
        functools.partial(_moba_sample_kernel, past_len=past_len),
        grid_spec=pltpu.PrefetchScalarGridSpec(
            num_scalar_prefetch=2,
            grid=(n_seq, N_HEADS, n_q),
            in_specs=[head(Q_PAD), head(PAGE_SIZE), head(PAGE_SIZE)] + page_specs + page_specs,
            out_specs=pl.BlockSpec((None, None, 1, HEAD_DIM), lambda b, h, q, i_, p_: (b, q, 0, h))),
        out_shape=jax.ShapeDtypeStruct((n_seq, n_q, 1, BRANCH_DIM), BF16),
        compiler_params=_params(3),
        name="moba_sample",
    )(ids, page_table, q_pad, k_new, v_new, *([kc] * (2 * MOBA_TOPK)), *([vc] * (2 * MOBA_TOPK)))
    return out.reshape(n_seq * n_q, BRANCH_DIM)


def _sb_sample_kernel(pt_ref, q_ref, kn_ref, vn_ref, carry_in_ref, acc_in_ref, *refs, past_len, first_page, with_new):
    pps = PAGES_PER_STEP
    k_refs, v_refs = refs[:pps], refs[pps:2 * pps]
    acc_ref, carry_ref, live_ref = refs[2 * pps:]
    step = pl.program_id(1)
    new_steps = 1 if with_new else 0
    heads = [slice(h * HEAD_DIM, (h + 1) * HEAD_DIM) for h in range(N_HEADS)]

    def keep(carry, acc, rows):
        carry_ref[rows, :] = jnp.broadcast_to(carry, acc.shape)
        acc_ref[rows, :] = acc

    @pl.when(step == 0)
    def _():
        if with_new:
            qpos = past_len + lax.broadcasted_iota(I32, (Q_PAD, 1), 0)
            allowed = (past_len + lax.broadcasted_iota(I32, (1, PAGE_SIZE), 1)) < qpos
            tri = _strict_lower(PAGE_SIZE)
            live = jnp.float32(SB_EXIT)
            for h, cols in enumerate(heads):
                carry, acc = _sb_block(q_ref[:, cols].astype(BF16), kn_ref[:, cols].astype(BF16),
                                       vn_ref[:, cols].astype(BF16), tri, allowed,
                                       jnp.zeros((Q_PAD, 1), F32), jnp.zeros((Q_PAD, HEAD_DIM), F32))
                keep(carry, acc, slice(h * Q_PAD, (h + 1) * Q_PAD))
                live = jnp.maximum(live, jnp.max(carry))
        else:
            carry_ref[...] = carry_in_ref[...]
            acc_ref[...] = acc_in_ref[...]
            live = jnp.max(carry_in_ref[...])
        live_ref[0] = (live > SB_EXIT).astype(I32)

    def visit_page(k_ref, v_ref, k0):
        q_stack = jnp.concatenate([q_ref[:, cols] for cols in heads], axis=0).astype(BF16)
        row = lax.broadcasted_iota(I32, (STACK, 1), 0)
        col = lax.broadcasted_iota(I32, (1, PAGE_ROWS), 1)
        allowed = ((col % N_HEADS) == (row // Q_PAD)) & ((k0 + col // N_HEADS) < (past_len + row % Q_PAD))
        r = lax.broadcasted_iota(I32, (PAGE_ROWS, PAGE_ROWS), 0) // N_HEADS
        c = lax.broadcasted_iota(I32, (PAGE_ROWS, PAGE_ROWS), 1) // N_HEADS
        later_token = jnp.where(r > c, 1.0, 0.0).astype(BF16)
        carry, acc = _sb_block(q_stack, _page_rows(k_ref).astype(BF16), _page_rows(v_ref).astype(BF16),
                               later_token, allowed, carry_ref[:, 0:1], acc_ref[...])
        keep(carry, acc, slice(None))
        live_ref[0] = (jnp.max(carry) > SB_EXIT).astype(I32)

    for k in range(pps):
        @pl.when((step >= new_steps) & (live_ref[0] > 0))
        def _(k=k):
            page = first_page - ((step - new_steps) * pps + k)
            visit_page(k_refs[k], v_refs[k], page * PAGE_SIZE)


def _sb_sample_call(page_table, q_pad, k_new, v_new, carry, acc, kc, vc, layer, *, past_len, first_page,
                    n_page_steps, with_new):
    n_seq = page_table.shape[0]
    new_steps = 1 if with_new else 0

    def page_map(b, s, pt_ref, *, k):
        return (layer, pt_ref[b, first_page - (jnp.maximum(s - new_steps, 0) * PAGES_PER_STEP + k)], 0, 0, 0)

    page_specs = [pl.BlockSpec(_PAGE_BLOCK, functools.partial(page_map, k=k)) for k in range(PAGES_PER_STEP)]
    whole = lambda rows, width: pl.BlockSpec((None, rows, width), lambda b, s, pt: (b, 0, 0))
    state = whole(STACK, HEAD_DIM)
    state_shape = jax.ShapeDtypeStruct((n_seq, STACK, HEAD_DIM), F32)
    return pl.pallas_call(
        functools.partial(_sb_sample_kernel, past_len=past_len, first_page=first_page, with_new=with_new),
        grid_spec=pltpu.PrefetchScalarGridSpec(
            num_scalar_prefetch=1,
            grid=(n_seq, new_steps + n_page_steps),
            in_specs=[whole(Q_PAD, BRANCH_DIM), whole(PAGE_SIZE, BRANCH_DIM), whole(PAGE_SIZE, BRANCH_DIM),
                      state, state] + page_specs + page_specs,
            out_specs=(state, state),
            scratch_shapes=[pltpu.SMEM((1,), I32)]),
        out_shape=(state_shape, state_shape),
        compiler_params=_params(2),
        name="sb_sample",
    )(page_table, q_pad, k_new, v_new, carry, acc, *([kc] * PAGES_PER_STEP), *([vc] * PAGES_PER_STEP))


def sb_sample(page_table, q_pad, k_new, v_new, k_cache, v_cache, layer, *, past_len):
    n_seq, n_pages = page_table.shape
    kc, vc = _paged(k_cache), _paged(v_cache)
    zeros = jnp.zeros((n_seq, STACK, HEAD_DIM), F32)
    args = (page_table, q_pad, k_new, v_new)
    acc, carry = _sb_sample_call(*args, zeros, zeros, kc, vc, layer, past_len=past_len, first_page=n_pages - 1,
                                 n_page_steps=1, with_new=True)
    older = n_pages - PAGES_PER_STEP
    walk_older = lambda: _sb_sample_call(*args, carry, acc, kc, vc, layer, past_len=past_len, first_page=older - 1,
                                         n_page_steps=older // PAGES_PER_STEP, with_new=False)[0]
    acc = lax.cond(jnp.max(carry) > SB_EXIT, walk_older, lambda: acc)
    out = acc.reshape(n_seq, N_HEADS, Q_PAD, HEAD_DIM).transpose(0, 2, 1, 3)
    return out.reshape(n_seq, Q_PAD, BRANCH_DIM).astype(BF16)


def _gate_mix_kernel(u_ref, ha_ref, hb_ref, hc_ref, hd_ref, *refs):
    wg_refs, wp_refs = refs[:N_BRANCH], refs[N_BRANCH:2 * N_BRANCH]
    o_ref, wgb_ref, wpb_ref = refs[2 * N_BRANCH:]

    @pl.when(pl.program_id(1) == 0)
    def _():
        for b in range(N_BRANCH):
            wgb_ref[b] = wg_refs[b][...].astype(BF16)
            wpb_ref[b] = wp_refs[b][...].astype(BF16)

    u = u_ref[...]
    acc = None
    for b, h_ref in enumerate((ha_ref, hb_ref, hc_ref, hd_ref)):
        term = jax.nn.sigmoid(_dot(u, wgb_ref[b])) * _dot(h_ref[...], wpb_ref[b])
        acc = term if acc is None else acc + term
    o_ref[...] = acc.astype(o_ref.dtype)


def gate_mix(u, hs, w_in, projs, layer):
    m = u.shape[0]
    tm, tn = min(m, 1024), 256
    gate_specs = [pl.BlockSpec((None, D_MODEL, tn),
                               functools.partial(lambda j, i, b: (layer, 0, (N_PROJ + b * D_MODEL) // tn + j), b=b))
                  for b in range(N_BRANCH)]
    proj_specs = [pl.BlockSpec((None, BRANCH_DIM, tn), lambda j, i: (layer, 0, j))] * N_BRANCH
    h_spec = pl.BlockSpec((tm, BRANCH_DIM), lambda j, i: (i, 0))
    return pl.pallas_call(
        _gate_mix_kernel,
        grid=(D_MODEL // tn, m // tm),
        in_specs=[pl.BlockSpec((tm, D_MODEL), lambda j, i: (i, 0))] + [h_spec] * N_BRANCH + gate_specs + proj_specs,
        out_specs=pl.BlockSpec((tm, tn), lambda j, i: (i, j)),
        out_shape=jax.ShapeDtypeStruct((m, D_MODEL), BF16),
        scratch_shapes=[pltpu.VMEM((N_BRANCH, D_MODEL, tn), BF16), pltpu.VMEM((N_BRANCH, BRANCH_DIM, tn), BF16)],
        compiler_params=_params(2),
        name="gate_mix",
    )(u, *hs, *([w_in] * N_BRANCH), *projs)


def _new_weights(te_ref, i):
    prev = te_ref[jnp.maximum(i - 1, 0)]
    return (i == 0) | (te_ref[i] != prev)


def _ffn_up_kernel(te_ref, nv_ref, xa_ref, xb_ref, w1_ref, w3_ref, o_ref, w1b_ref, w3b_ref):
    i = pl.program_id(1)

    @pl.when(_new_weights(te_ref, i))
    def _():
        w1b_ref[...] = w1_ref[...].astype(BF16)
        w3b_ref[...] = w3_ref[...].astype(BF16)

    @pl.when(i < nv_ref[0])
    def _():
        half = xa_ref.shape[1]
        xa, xb = xa_ref[...], xb_ref[...]
        a = _dot(xa, w1b_ref[:half]) + _dot(xb, w1b_ref[half:])
        g = _dot(xa, w3b_ref[:half]) + _dot(xb, w3b_ref[half:])
        o_ref[...] = (a * jax.nn.sigmoid(a) * g).astype(o_ref.dtype)

    @pl.when(i >= nv_ref[0])
    def _():
        o_ref[...] = jnp.zeros_like(o_ref)


def ffn_up(xa, xb, w1, w3, tile_expert, n_valid, tm, cols=(0, 0)):
    r = xa.shape[0]
    half = D_MODEL // 2
    tn = 512
    row = lambda c: (lambda j, i, te, nv: (jnp.minimum(i, nv[0] - 1), c))
    wmap = lambda j, i, te, nv: (te[i], 0, j)
    return pl.pallas_call(
        _ffn_up_kernel,
        grid_spec=pltpu.PrefetchScalarGridSpec(
            num_scalar_prefetch=2,
            grid=(D_FF // tn, r // tm),
            in_specs=[pl.BlockSpec((tm, half), row(cols[0])), pl.BlockSpec((tm, half), row(cols[1])),
                      pl.BlockSpec((None, D_MODEL, tn), wmap), pl.BlockSpec((None, D_MODEL, tn), wmap)],
            out_specs=pl.BlockSpec((tm, tn), lambda j, i, te, nv: (i, j)),
            scratch_shapes=[pltpu.VMEM((D_MODEL, tn), BF16), pltpu.VMEM((D_MODEL, tn), BF16)]),
        out_shape=jax.ShapeDtypeStruct((r, D_FF), BF16),
        compiler_params=_params(2),
        name="ffn_up",
    )(tile_expert, n_valid, xa, xb, w1, w3)


def _ffn_down_kernel(te_ref, nv_ref, h_ref, w_ref, o_ref, wb_ref):
    i = pl.program_id(1)

    @pl.when(_new_weights(te_ref, i))
    def _():
        wb_ref[...] = w_ref[...].astype(BF16)

    @pl.when(i < nv_ref[0])
    def _():
        o_ref[...] = _dot(h_ref[...], wb_ref[...])

    @pl.when(i >= nv_ref[0])
    def _():
        o_ref[...] = jnp.zeros_like(o_ref)


def ffn_down(h, w2, tile_expert, n_valid, tm):
    r = h.shape[0]
    tn = 512
    return pl.pallas_call(
        _ffn_down_kernel,
        grid_spec=pltpu.PrefetchScalarGridSpec(
            num_scalar_prefetch=2,
            grid=(D_MODEL // tn, r // tm),
            in_specs=[pl.BlockSpec((tm, D_FF), lambda j, i, te, nv: (jnp.minimum(i, nv[0] - 1), 0)),
                      pl.BlockSpec((None, D_FF, tn), lambda j, i, te, nv: (te[i], 0, j))],
            out_specs=pl.BlockSpec((tm, tn), lambda j, i, te, nv: (i, j)),
            scratch_shapes=[pltpu.VMEM((D_FF, tn), BF16)]),
        out_shape=jax.ShapeDtypeStruct((r, D_MODEL), F32),
        compiler_params=_params(2),
        name="ffn_down",
    )(tile_expert, n_valid, h, w2)


def dense_ffn(u, w1, w3, w2, j):
    m = u.shape[0]
    tiles = lambda tm: (jnp.full((m // tm,), j, I32), jnp.full((1,), m // tm, I32), tm)
    h = ffn_up(u, u, w1, w3, *tiles(min(m, 1024)), cols=(0, 1))
    return ffn_down(h, w2, *tiles(min(m, 512)))


MOE_TM = 512


def _moe_gather_kernel(src_ref, x_ref, xa_ref, xb_ref, buf_ref, *, tm):
    base = pl.program_id(1) * tm

    def body(r, _):
        buf_ref[pl.ds(r, 1), :] = x_ref[pl.ds(src_ref[base + r], 1), :]
        return 0

    lax.fori_loop(0, tm, body, 0, unroll=8)
    w = buf_ref[...]
    xa_ref[...] = pltpu.bitcast(w & jnp.uint32(0xFFFF0000), F32).astype(BF16)
    xb_ref[...] = pltpu.bitcast(w << 16, F32).astype(BF16)


def moe_gather(packed, src_tok, n_rows):
    t, half = packed.shape
    tn = half // 2
    return pl.pallas_call(
        functools.partial(_moe_gather_kernel, tm=MOE_TM),
        grid_spec=pltpu.PrefetchScalarGridSpec(
            num_scalar_prefetch=1,
            grid=(half // tn, n_rows // MOE_TM),
            in_specs=[pl.BlockSpec((t, tn), lambda c, g, src: (0, c))],
            out_specs=(pl.BlockSpec((MOE_TM, tn), lambda c, g, src: (g, c)),
                       pl.BlockSpec((MOE_TM, tn), lambda c, g, src: (g, c))),
            scratch_shapes=[pltpu.VMEM((MOE_TM, tn), jnp.uint32)]),
        out_shape=(jax.ShapeDtypeStruct((n_rows, half), BF16), jax.ShapeDtypeStruct((n_rows, half), BF16)),
        compiler_params=_params(2),
        name="moe_gather",
    )(src_tok, packed)


COMBINE_TM = 256


def _moe_combine_kernel(p0_ref, p1_ref, rt_ref, y_hbm, o_ref, a_ref, b_ref, sem, *, tm):
    base = pl.program_id(0) * tm

    def row_copy(pos_ref, buf_ref, slot, r):
        return pltpu.make_async_copy(y_hbm.at[pl.ds(pos_ref[base + r], 1)], buf_ref.at[pl.ds(r, 1)], sem.at[slot])

    def start(r, _):
        row_copy(p0_ref, a_ref, 0, r).start()
        row_copy(p1_ref, b_ref, 1, r).start()
        return 0

    def wait(r, _):
        row_copy(p0_ref, a_ref, 0, r).wait()
        row_copy(p1_ref, b_ref, 1, r).wait()
        return 0

    lax.fori_loop(0, tm, start, 0, unroll=8)
    lax.fori_loop(0, tm, wait, 0, unroll=8)
    o_ref[...] = rt_ref[:, 2:3] * a_ref[...] + rt_ref[:, 3:4] * b_ref[...]


def moe_combine(y_sorted, pos0, pos1, route):
    n_tok = route.shape[0]
    tm = COMBINE_TM
    return pl.pallas_call(
        functools.partial(_moe_combine_kernel, tm=tm),
        grid_spec=pltpu.PrefetchScalarGridSpec(
            num_scalar_prefetch=2,
            grid=(n_tok // tm,),
            in_specs=[pl.BlockSpec((tm, LANES), lambda i, p0, p1: (i, 0)),
                      pl.BlockSpec(memory_space=pl.ANY)],
            out_specs=pl.BlockSpec((tm, D_MODEL), lambda i, p0, p1: (i, 0)),
            scratch_shapes=[pltpu.VMEM((tm, D_MODEL), F32), pltpu.VMEM((tm, D_MODEL), F32),
                            pltpu.SemaphoreType.DMA((2,))]),
        out_shape=jax.ShapeDtypeStruct((n_tok, D_MODEL), F32),
        compiler_params=_params(1),
        name="moe_combine",
    )(pos0, pos1, route, y_sorted)


def moe_ffn(packed, route, w1, w3, w2, j):
    t = packed.shape[0]
    n_pairs = 2 * t
    tm = MOE_TM
    n_tiles = -(-(n_pairs + N_EXPERTS * (tm - 1)) // tm)
    n_rows = n_tiles * tm
    expert = route[:, :2].astype(I32).reshape(n_pairs)
    onehot = (expert[:, None] == jnp.arange(N_EXPERTS, dtype=I32)[None, :]).astype(I32)
    running = jnp.cumsum(onehot, axis=0)
    counts = running[-1]
    rank = jnp.take_along_axis(running, expert[:, None], axis=1)[:, 0] - 1
    padded = ((counts + tm - 1) // tm) * tm
    ends = jnp.cumsum(padded)
    starts = ends - padded
    dest = starts[expert] + rank
    src_tok = jnp.zeros((n_rows,), I32).at[dest].set(jnp.arange(n_pairs, dtype=I32) // 2)
    n_valid = (ends[-1] // tm).astype(I32).reshape(1)
    tile_start = jnp.arange(n_tiles, dtype=I32) * tm
    tile_expert = jnp.minimum(jnp.sum((ends[None, :] <= tile_start[:, None]).astype(I32), axis=1), N_EXPERTS - 1)
    last_expert = tile_expert[jnp.maximum(n_valid[0] - 1, 0)]
    tile_expert = jnp.where(tile_start < ends[-1], tile_expert, last_expert) + j * N_EXPERTS
    flat = lambda w: w.reshape((-1,) + w.shape[2:])

    xa, xb = moe_gather(packed, src_tok, n_rows)
    h = ffn_up(xa, xb, flat(w1), flat(w3), tile_expert, n_valid, tm)
    y = ffn_down(h, flat(w2), tile_expert, n_valid, tm)
    pad = -t % COMBINE_TM
    pos = jnp.pad(dest.reshape(t, 2), ((0, pad), (0, 0)))
    return moe_combine(y, pos[:, 0], pos[:, 1], jnp.pad(route, ((0, pad), (0, 0))))


def kernel(x_prompt, x_sample, cache_moba_k, cache_moba_v, cache_sb_k, cache_sb_v, state_pool, state_conv,
           page_table, c_prompt, c_sample, w_ada, b_ada, w_in, pool_w, pool_scale, pool_proj, moba_proj,
           conv_w, conv_b, conv_ln_g, conv_ln_b, conv_pw, sb_proj, w_out, ln1_g, ln1_b, ln2_g, ln2_b,
           ffn_w1, ffn_w3, ffn_w2, moe_router, moe_router_b, moe_w1, moe_w3, moe_w2):
    bp, lp, _ = x_prompt.shape
    bs, ls, _ = x_sample.shape
    mp, ms = bp * lp, bs * ls
    past_len = page_table.shape[1] * PAGE_SIZE
    assert past_len // MOBA_BLOCK >= MOBA_TOPK and ls <= Q_PAD and lp % MOBA_BLOCK == 0

    c_all = jnp.zeros((16, D_MODEL), F32).at[:bp].set(c_prompt).at[bp:bp + bs].set(c_sample)
    mod = ada_mod(c_all, w_ada, b_ada).reshape(DEPTH, 16, 6, D_MODEL)

    def mods(l, k):
        mp_ = mod[l, :bp, k].reshape(bp, 1, D_MODEL)
        ms_ = jnp.repeat(mod[l, bp:bp + bs, k], ls, axis=0).reshape(1, ms, D_MODEL)
        return mp_, ms_

    pos_p = jnp.arange(lp, dtype=I32)
    pos_s = past_len + jnp.arange(ls, dtype=I32)
    tab_p, tab_s = rope_tables(pos_p), rope_tables(pos_s)

    xp = x_prompt.reshape(mp, D_MODEL)
    xs = x_sample.reshape(ms, D_MODEL)
    sc1, sh1 = mods(0, 1), mods(0, 0)
    up = ln_mod(xp, sc1[0], sh1[0], lp)
    us = ln_mod(xs, sc1[1], sh1[1], ls)
    zero_pool = jnp.zeros((bp, POOL_HALO, BRANCH_DIM), F32)
    zero_conv = jnp.zeros((bp, CONV_HALO, BRANCH_DIM), F32)
    outs_p = [[] for _ in range(6)]
    outs_s = [[] for _ in range(6)]

    def head4(t, b, l):
        return t.reshape(b, l, N_HEADS, HEAD_DIM)

    for l in range(DEPTH):
        g1p, g1s = mods(l, 2)
        sc2, sh2 = mods(l, 4), mods(l, 3)
        g2p, g2s = mods(l, 5)
        projs = (pool_proj, moba_proj, conv_pw, sb_proj)

        p3 = matmul(up, w_in, l, n=N_PROJ, tn=768).reshape(bp, lp, N_PROJ)
        ha = pool_branch(p3, zero_pool, pool_w[l], pool_scale[l], pos0=0)
        hc, glu = conv_branch(p3, zero_conv, conv_w[l], conv_b[l], conv_ln_g[l], conv_ln_b[l])
        q_rot, k_rot, k_mean = rope_qk(p3, tab_p, with_means=True)
        hb = moba_prompt(q_rot, k_rot, p3, k_mean)
        hd = sb_prompt(p3)
        flat_p = lambda t: t.reshape(mp, BRANCH_DIM)
        mix = gate_mix(up, (flat_p(ha), flat_p(hb), flat_p(hc), flat_p(hd)), w_in, projs, l)
        attn = matmul(mix, w_out, l)
        outs_p[0].append(head4(k_rot, bp, lp))
        outs_p[1].append(head4(p3[:, :, 3 * BRANCH_DIM:4 * BRANCH_DIM], bp, lp))
        outs_p[2].append(head4(p3[:, :, 7 * BRANCH_DIM:8 * BRANCH_DIM], bp, lp))
        outs_p[3].append(head4(p3[:, :, 8 * BRANCH_DIM:9 * BRANCH_DIM], bp, lp))
        outs_p[4].append(p3[:, lp - POOL_BUF:, :BRANCH_DIM])
        outs_p[5].append(glu[:, lp - CONV_BUF:])

        ps3 = matmul(us, w_in, l, n=N_PROJ).reshape(bs, ls, N_PROJ)
        pool_hist = jnp.concatenate([state_pool[l], ps3[:, :, :BRANCH_DIM]], axis=1)
        pool_init = jnp.pad(state_pool[l], ((0, 0), (POOL_HALO - POOL_BUF, 0), (0, 0)))
        has = pool_branch(ps3, pool_init, pool_w[l], pool_scale[l], pos0=past_len)
        conv_init = jnp.pad(state_conv[l], ((0, 0), (CONV_HALO - CONV_BUF, 0), (0, 0)))
        hcs, glu_s = conv_branch(ps3, conv_init, conv_w[l], conv_b[l], conv_ln_g[l], conv_ln_b[l])
        conv_hist = jnp.concatenate([state_conv[l], glu_s], axis=1)
        qs_rot, ks_rot = rope_qk(ps3, tab_s, with_means=False)
        pad_q = lambda t: jnp.pad(t, ((0, 0), (0, Q_PAD - ls), (0, 0)))
        pad_kv = lambda t: jnp.pad(t, ((0, 0), (0, PAGE_SIZE - ls), (0, 0)))
        mv_s = ps3[:, :, 3 * BRANCH_DIM:4 * BRANCH_DIM]
        sq_s, sk_s, sv_s = (ps3[:, :, k * BRANCH_DIM:(k + 1) * BRANCH_DIM] for k in (6, 7, 8))
        k_mean_s = page_block_means(cache_moba_k, l, page_table)
        picks = moba_select(pad_q(qs_rot), k_mean_s)
        ids = picks[:, :ls, :N_HEADS * MOBA_TOPK].reshape(-1)
        hbs = moba_sample(ids, page_table, pad_q(qs_rot), pad_kv(ks_rot), pad_kv(mv_s),
                          cache_moba_k, cache_moba_v, l, n_q=ls, past_len=past_len)
        hds = sb_sample(page_table, pad_q(sq_s), pad_kv(sk_s), pad_kv(sv_s), cache_sb_k, cache_sb_v, l,
                        past_len=past_len)[:, :ls].reshape(ms, BRANCH_DIM)
        flat_s = lambda t: t.reshape(ms, BRANCH_DIM)
        mix_s = gate_mix(us, (flat_s(has), hbs, flat_s(hcs), hds), w_in, projs, l)
        attn_s = matmul(mix_s, w_out, l)
        outs_s[0].append(head4(ks_rot, bs, ls))
        outs_s[1].append(head4(mv_s, bs, ls))
        outs_s[2].append(head4(sk_s, bs, ls))
        outs_s[3].append(head4(sv_s, bs, ls))
        outs_s[4].append(pool_hist[:, -POOL_BUF:])
        outs_s[5].append(conv_hist[:, -CONV_BUF:])

        j = l // 2
        if l % 2 == 0:
            xp, u2p = resid_ln(xp, attn, g1p, ln1_g[l], ln1_b[l], lp, sc2[0], sh2[0])
            xs, u2s = resid_ln(xs, attn_s, g1s, ln1_g[l], ln1_b[l], ls, sc2[1], sh2[1])
            fp = dense_ffn(u2p, ffn_w1, ffn_w3, ffn_w2, j)
            fs = dense_ffn(u2s, ffn_w1, ffn_w3, ffn_w2, j)
        else:
            router = (moe_router[j], moe_router_b[j])
            xp, pk_p, rt_p = resid_ln(xp, attn, g1p, ln1_g[l], ln1_b[l], lp, sc2[0], sh2[0], router)
            xs, pk_s, rt_s = resid_ln(xs, attn_s, g1s, ln1_g[l], ln1_b[l], ls, sc2[1], sh2[1], router)
            f_all = moe_ffn(jnp.concatenate([pk_p, pk_s], axis=0), jnp.concatenate([rt_p, rt_s], axis=0),
                            moe_w1, moe_w3, moe_w2, j)
            fp, fs = f_all, f_all
        fs_row0 = 0 if l % 2 == 0 else mp
        if l + 1 < DEPTH:
            nsc, nsh = mods(l + 1, 1), mods(l + 1, 0)
            xp, up = resid_ln(xp, fp, g2p, ln2_g[l], ln2_b[l], lp, nsc[0], nsh[0])
            xs, us = resid_ln(xs, fs, g2s, ln2_g[l], ln2_b[l], ls, nsc[1], nsh[1], y_row0=fs_row0)
        else:
            xp = resid_ln(xp, fp, g2p, ln2_g[l], ln2_b[l], lp)
            xs = resid_ln(xs, fs, g2s, ln2_g[l], ln2_b[l], ls, y_row0=fs_row0)

    stack = lambda group: tuple(jnp.stack(t) for t in group)
    return (xp.reshape(bp, lp, D_MODEL), xs.reshape(bs, ls, D_MODEL)) + stack(outs_p) + stack(outs_s)
```
